```python
import jax, jax.numpy as jnp
from jax import lax
import numpy as np

D_MODEL = 2048
BATCH = 4
SEQ = 2048
DEPTH = 1
DEC_BATCH = 8
DEC_SEQ = 1
PAST_LEN = 16384
PAGE_SIZE = 128

HEAD_DIM = 128
N_HEADS = D_MODEL // HEAD_DIM
H_SB = N_HEADS // 2
H_FOX = N_HEADS - H_SB
D_SB = H_SB * HEAD_DIM
D_FOX = H_FOX * HEAD_DIM
D_IN = 3 * D_SB + 3 * D_FOX + H_FOX
SCALE = HEAD_DIM ** -0.5
Q_BLOCK = 128
N_GROUPS = 4
EXPERTS_PER_GROUP = 8
N_EXPERTS = N_GROUPS * EXPERTS_PER_GROUP
D_EXPERT = D_MODEL // 8
TOP_K_EXPERT = 2
RMS_EPS = 1e-6

kernel_name = "hymba_style_stickbreak_fox_hmoe_step"


def rms_norm(x, g):
    xf = x.astype(jnp.float32)
    y = xf * lax.rsqrt(jnp.mean(xf * xf, axis=-1, keepdims=True) + RMS_EPS)
    return (y * g.astype(jnp.float32)).astype(x.dtype)


def project(u, w_in, b_forget):
    B, S, _ = u.shape
    z = jnp.einsum('bsd,de->bse', u, w_in)
    cuts = [int(c) for c in np.cumsum([D_SB, D_SB, D_SB, D_FOX, D_FOX, D_FOX])]
    q_sb, k_sb, v_sb, q_fx, k_fx, v_fx, f = jnp.split(z, cuts, axis=-1)
    sb = lambda t: t.reshape(B, S, H_SB, HEAD_DIM)
    fx = lambda t: t.reshape(B, S, H_FOX, HEAD_DIM)
    logf = jax.nn.log_sigmoid((f + b_forget).astype(jnp.float32))
    return sb(q_sb), sb(k_sb), sb(v_sb), fx(q_fx), fx(k_fx), fx(v_fx), logf


def stick_breaking(q, k, v, q_pos, k_pos):
    z = jnp.einsum('bqhd,bkhd->bhqk', q, k, preferred_element_type=jnp.float32) * SCALE
    causal = k_pos[None, :] < q_pos[:, None]
    log_stay = jnp.where(causal, jax.nn.log_sigmoid(-z), 0.0)
    later = lax.cumsum(log_stay, axis=3, reverse=True) - log_stay
    a = jnp.where(causal, jnp.exp(jax.nn.log_sigmoid(z) + later), 0.0)
    return jnp.einsum('bhqk,bkhd->bqhd', a.astype(v.dtype), v)


def forgetting(q, k, v, cq, ck, q_pos, k_pos):
    z = jnp.einsum('bqhd,bkhd->bhqk', q, k, preferred_element_type=jnp.float32) * SCALE
    z = z + cq[..., :, None] - ck[..., None, :]
    causal = k_pos[None, :] <= q_pos[:, None]
    p = jax.nn.softmax(jnp.where(causal, z, -jnp.inf), axis=-1)
    return jnp.einsum('bhqk,bkhd->bqhd', p.astype(v.dtype), v)


def merge_heads(o_sb, o_fx, g_sb, g_fx, w_out):
    B, S = o_sb.shape[:2]
    o = jnp.concatenate([rms_norm(o_sb.reshape(B, S, D_SB), g_sb),
                         rms_norm(o_fx.reshape(B, S, D_FOX), g_fx)], axis=-1)
    return jnp.einsum('bse,ed->bsd', o, w_out)


def prompt_attention(u, w_in, b_forget, g_sb, g_fx, w_out):
    B, S, _ = u.shape
    q_sb, k_sb, v_sb, q_fx, k_fx, v_fx, logf = project(u, w_in, b_forget)
    c = jnp.cumsum(logf, axis=1).transpose(0, 2, 1)
    pos = jnp.arange(S)
    nb = S // Q_BLOCK
    blocks = lambda t: t.reshape(B, nb, Q_BLOCK, *t.shape[2:]).swapaxes(0, 1)
    cq_b = c.reshape(B, H_FOX, nb, Q_BLOCK).transpose(2, 0, 1, 3)
    pos_b = pos.reshape(nb, Q_BLOCK)

    def body(xs):
        qs, qf, cq, qp = xs
        return (stick_breaking(qs, k_sb, v_sb, qp, pos),
                forgetting(qf, k_fx, v_fx, cq, c, qp, pos))

    o_sb, o_fx = lax.map(body, (blocks(q_sb), blocks(q_fx), cq_b, pos_b))
    unblock = lambda t: t.swapaxes(0, 1).reshape(B, S, *t.shape[3:])
    y = merge_heads(unblock(o_sb), unblock(o_fx), g_sb, g_fx, w_out)
    return y, (k_sb, v_sb, k_fx, v_fx, logf)


def sample_attention(u, c_sb_k, c_sb_v, c_fx_k, c_fx_v, c_logf, page_table,
                     w_in, b_forget, g_sb, g_fx, w_out):
    B, T, _ = u.shape
    past = page_table.shape[1] * c_sb_k.shape[1]
    q_sb, k_sb, v_sb, q_fx, k_fx, v_fx, logf = project(u, w_in, b_forget)
    gather = lambda cache: cache[page_table].reshape(B, past, *cache.shape[2:])
    k_sb_all = jnp.concatenate([gather(c_sb_k), k_sb], axis=1)
    v_sb_all = jnp.concatenate([gather(c_sb_v), v_sb], axis=1)
    k_fx_all = jnp.concatenate([gather(c_fx_k), k_fx], axis=1)
    v_fx_all = jnp.concatenate([gather(c_fx_v), v_fx], axis=1)
    logf_all = jnp.concatenate([gather(c_logf).astype(jnp.float32), logf], axis=1)
    c = jnp.cumsum(logf_all, axis=1).transpose(0, 2, 1)
    k_pos = jnp.arange(past + T)
    q_pos = past + jnp.arange(T)
    o_sb = stick_breaking(q_sb, k_sb_all, v_sb_all, q_pos, k_pos)
    o_fx = forgetting(q_fx, k_fx_all, v_fx_all, c[:, :, past:], c, q_pos, k_pos)
    y = merge_heads(o_sb, o_fx, g_sb, g_fx, w_out)
    return y, (k_sb, v_sb, k_fx, v_fx, logf)


def hier_moe(u, w_rg, b_rg, w_re, b_re, w_gate, w_up, w_down):
    uf = u.astype(jnp.float32)
    lg = uf @ w_rg.astype(jnp.float32) + b_rg.astype(jnp.float32)
    p_group = jax.nn.softmax(lg, axis=-1)
    _, gsel = lax.top_k(lg, 1)
    g1 = jnp.take_along_axis(p_group, gsel, axis=-1)
    le = jnp.einsum('td,gde->tge', uf, w_re.astype(jnp.float32)) + b_re.astype(jnp.float32)
    le_sel = jnp.take_along_axis(le, gsel[:, :, None], axis=1)[:, 0]
    v2, i2 = lax.top_k(le_sel, TOP_K_EXPERT)
    gate = g1 * jax.nn.softmax(v2, axis=-1)
    eid = gsel * EXPERTS_PER_GROUP + i2
    comb = jnp.sum(jax.nn.one_hot(eid, N_EXPERTS, dtype=jnp.float32) * gate[..., None], axis=1)
    h = jax.nn.silu(jnp.einsum('td,edf->tef', u, w_gate)) * jnp.einsum('td,edf->tef', u, w_up)
    h = h * comb[:, :, None].astype(h.dtype)
    return jnp.einsum('tef,efd->td', h, w_down)


def setup_inputs(seed: int = 0) -> dict:
    key = jax.random.key(seed)
    ks = jax.random.split(key, 24)
    n_pages = PAST_LEN // PAGE_SIZE
    n_used = DEC_BATCH * n_pages
    n_phys = n_used + max(1, n_used // 4)
    nrm = lambda k, shape, s=1.0: jax.random.normal(k, shape, jnp.float32) * s
    ds = D_MODEL ** -0.5
    page_table = jax.random.permutation(ks[0], n_phys)[:n_used].astype(jnp.int32).reshape(DEC_BATCH, n_pages)
    return {
        "x_prompt": nrm(ks[1], (BATCH, SEQ, D_MODEL)),
        "x_sample": nrm(ks[2], (DEC_BATCH, DEC_SEQ, D_MODEL)),
        "cache_sb_k": nrm(ks[3], (DEPTH, n_phys, PAGE_SIZE, H_SB, HEAD_DIM)),
        "cache_sb_v": nrm(ks[4], (DEPTH, n_phys, PAGE_SIZE, H_SB, HEAD_DIM)),
        "cache_fox_k": nrm(ks[5], (DEPTH, n_phys, PAGE_SIZE, H_FOX, HEAD_DIM)),
        "cache_fox_v": nrm(ks[6], (DEPTH, n_phys, PAGE_SIZE, H_FOX, HEAD_DIM)),
        "cache_fox_logf": jax.nn.log_sigmoid(3.0 + nrm(ks[7], (DEPTH, n_phys, PAGE_SIZE, H_FOX), 0.5)),
        "page_table": page_table,
        "norm_attn_g": 1.0 + nrm(ks[8], (DEPTH, D_MODEL), 0.02),
        "w_in": nrm(ks[9], (DEPTH, D_MODEL, D_IN), ds),
        "b_forget": 3.0 + nrm(ks[10], (DEPTH, H_FOX), 0.5),
        "g_sb_out": 1.0 + nrm(ks[11], (DEPTH, D_SB), 0.02),
        "g_fox_out": 1.0 + nrm(ks[12], (DEPTH, D_FOX), 0.02),
        "w_out": nrm(ks[13], (DEPTH, D_MODEL, D_MODEL), ds),
        "norm_ffn_g": 1.0 + nrm(ks[14], (DEPTH, D_MODEL), 0.02),
        "w_router_group": nrm(ks[15], (DEPTH, D_MODEL, N_GROUPS), ds),
        "b_router_group": nrm(ks[16], (DEPTH, N_GROUPS), 0.01),
        "w_router_expert": nrm(ks[17], (DEPTH, N_GROUPS, D_MODEL, EXPERTS_PER_GROUP), ds),
        "b_router_expert": nrm(ks[18], (DEPTH, N_GROUPS, EXPERTS_PER_GROUP), 0.01),
        "w_expert_gate": nrm(ks[19], (DEPTH, N_EXPERTS, D_MODEL, D_EXPERT), ds),
        "w_expert_up": nrm(ks[20], (DEPTH, N_EXPERTS, D_MODEL, D_EXPERT), ds),
        "w_expert_down": nrm(ks[21], (DEPTH, N_EXPERTS, D_EXPERT, D_MODEL), D_EXPERT ** -0.5),
        "norm_final_g": 1.0 + nrm(ks[22], (D_MODEL,), 0.02),
    }


def reference(x_prompt, x_sample, cache_sb_k, cache_sb_v, cache_fox_k, cache_fox_v,
              cache_fox_logf, page_table, norm_attn_g, w_in, b_forget, g_sb_out,
              g_fox_out, w_out, norm_ffn_g, w_router_group, b_router_group,
              w_router_expert, b_router_expert, w_expert_gate, w_expert_up,
              w_expert_down, norm_final_g):
    B, S, D = x_prompt.shape
    DB, DS, _ = x_sample.shape
    hp, hs = x_prompt, x_sample
    new_p, new_s = [], []
    for l in range(DEPTH):
        yp, st_p = prompt_attention(rms_norm(hp, norm_attn_g[l]), w_in[l], b_forget[l],
                                    g_sb_out[l], g_fox_out[l], w_out[l])
        ys, st_s = sample_attention(rms_norm(hs, norm_attn_g[l]), cache_sb_k[l], cache_sb_v[l],
                                    cache_fox_k[l], cache_fox_v[l], cache_fox_logf[l], page_table,
                                    w_in[l], b_forget[l], g_sb_out[l], g_fox_out[l], w_out[l])
        hp = hp + yp
        hs = hs + ys
        u = jnp.concatenate([rms_norm(hp, norm_ffn_g[l]).reshape(B * S, D),
                             rms_norm(hs, norm_ffn_g[l]).reshape(DB * DS, D)], axis=0)
        f = hier_moe(u, w_router_group[l], b_router_group[l], w_router_expert[l],
                     b_router_expert[l], w_expert_gate[l], w_expert_up[l], w_expert_down[l])
        hp = hp + f[:B * S].reshape(B, S, D)
        hs = hs + f[B * S:].reshape(DB, DS, D)
        new_p.append(st_p)
        new_s.append(st_s)
    y_prompt = rms_norm(hp, norm_final_g)
    y_sample = rms_norm(hs, norm_final_g)
    sb_k_prompt = jnp.stack([s[0] for s in new_p])
    sb_v_prompt = jnp.stack([s[1] for s in new_p])
    fox_k_prompt = jnp.stack([s[2] for s in new_p])
    fox_v_prompt = jnp.stack([s[3] for s in new_p])
    fox_logf_prompt = jnp.stack([s[4] for s in new_p])
    sb_k_sample = jnp.stack([s[0] for s in new_s])
    sb_v_sample = jnp.stack([s[1] for s in new_s])
    fox_k_sample = jnp.stack([s[2] for s in new_s])
    fox_v_sample = jnp.stack([s[3] for s in new_s])
    fox_logf_sample = jnp.stack([s[4] for s in new_s])
    return (y_prompt, y_sample, sb_k_prompt, sb_v_prompt, fox_k_prompt, fox_v_prompt,
            fox_logf_prompt, sb_k_sample, sb_v_sample, fox_k_sample, fox_v_sample,
            fox_logf_sample)
```

```python
import functools

import jax
import jax.numpy as jnp
from jax import lax
from jax.experimental import pallas as pl
from jax.experimental.pallas import tpu as pltpu

HEAD_DIM = 128
N_SB = 8
N_FOX = 8
N_GROUPS = 4
EXPERTS_PER_GROUP = 8
N_EXPERTS = N_GROUPS * EXPERTS_PER_GROUP
SCALE = HEAD_DIM ** -0.5
RMS_EPS = 1e-6
LANES = 128
VMEM_LIMIT = 56 * 1024 * 1024
NEG_INF = float("-inf")
BF16 = jnp.bfloat16
F32 = jnp.float32


def _cparams(sem):
    return pltpu.CompilerParams(dimension_semantics=sem, vmem_limit_bytes=VMEM_LIMIT)


def _rms(x, g):
    return x * lax.rsqrt(jnp.mean(x * x, axis=-1, keepdims=True) + RMS_EPS) * g


def _log_sigmoid(x):
    return jnp.minimum(x, 0.0) - jnp.log(1.0 + jnp.exp(-jnp.abs(x)))


def _split_stack(x, terms=2):
    pieces = []
    rest = x
    for _ in range(terms - 1):
        head = rest.astype(BF16).astype(F32)
        pieces.append(head)
        rest = rest - head
    pieces.append(rest)
    return jnp.concatenate(pieces, axis=0).astype(BF16)


def _dot_nt(a, b):
    return lax.dot_general(a, b, (((1,), (1,)), ((), ())), preferred_element_type=F32)


def _inproj_kernel(x_ref, g_ref, w_ref, wf_ref, bf_ref, z_ref, logf_ref, u_ref):
    j = pl.program_id(1)

    @pl.when(j == 0)
    def _():
        u = _rms(x_ref[...], g_ref[...]).astype(BF16)
        u_ref[...] = u
        f = jnp.dot(u, wf_ref[...], preferred_element_type=F32) + bf_ref[...]
        logf_ref[...] = _log_sigmoid(f)

    z_ref[...] = jnp.dot(u_ref[...], w_ref[...], preferred_element_type=F32)


def _inproj(x, g, w_main, w_f, b_f, tm):
    t, d = x.shape
    n_out = w_main.shape[1] // 1024
    return pl.pallas_call(
        _inproj_kernel,
        grid=(t // tm, n_out),
        in_specs=[
            pl.BlockSpec((tm, d), lambda i, j: (i, 0)),
            pl.BlockSpec((1, d), lambda i, j: (0, 0)),
            pl.BlockSpec((d, 1024), lambda i, j: (0, j)),
            pl.BlockSpec((d, LANES), lambda i, j: (0, 0)),
            pl.BlockSpec((1, LANES), lambda i, j: (0, 0)),
        ],
        out_specs=[
            pl.BlockSpec((None, tm, 1024), lambda i, j: (j, i, 0)),
            pl.BlockSpec((tm, LANES), lambda i, j: (i, 0)),
        ],
        out_shape=[
            jax.ShapeDtypeStruct((n_out, t, 1024), F32),
            jax.ShapeDtypeStruct((t, LANES), F32),
        ],
        scratch_shapes=[pltpu.VMEM((tm, d), BF16)],
        compiler_params=_cparams(("arbitrary", "arbitrary")),
        name="inproj",
    )(x, g, w_main, w_f, b_f)


def _cumsum_kernel(x_ref, c_ref, *, chunk):
    s = x_ref.shape[-1]
    r = lax.broadcasted_iota(jnp.int32, (chunk, chunk), 0)
    c = lax.broadcasted_iota(jnp.int32, (chunk, chunk), 1)
    upper = (r <= c).astype(BF16)
    carry = jnp.zeros((x_ref.shape[0], 1), F32)
    for k in range(s // chunk):
        x = x_ref[:, k * chunk:(k + 1) * chunk]
        parts = jnp.dot(_split_stack(x, 3), upper, preferred_element_type=F32)
        n = x.shape[0]
        cs = parts[:n] + parts[n:2 * n] + parts[2 * n:] + carry
        c_ref[:, k * chunk:(k + 1) * chunk] = cs
        carry = cs[:, chunk - 1:chunk]


def _cumsum_rows(x):
    b, h, s = x.shape
    return pl.pallas_call(
        functools.partial(_cumsum_kernel, chunk=256),
        grid=(b,),
        in_specs=[pl.BlockSpec((None, h, s), lambda i: (i, 0, 0))],
        out_specs=pl.BlockSpec((None, h, s), lambda i: (i, 0, 0)),
        out_shape=jax.ShapeDtypeStruct((b, h, s), F32),
        compiler_params=_cparams(("arbitrary",)),
        name="logf_cumsum",
    )(x)


def _sb_kernel(q_ref, k_ref, v_ref, o_ref, *, tq, tk):
    i = pl.program_id(2)
    q = q_ref[...].astype(BF16)
    r = lax.broadcasted_iota(jnp.int32, (tk, tk), 0)
    c = lax.broadcasted_iota(jnp.int32, (tk, tk), 1)
    after = (r > c).astype(BF16)
    strictly_causal = lax.broadcasted_iota(jnp.int32, (tq, tk), 1) < lax.broadcasted_iota(jnp.int32, (tq, tk), 0)

    def block(j, run, acc, diag):
        start = pl.multiple_of(j * tk, tk)
        k = k_ref[pl.ds(start, tk), :].astype(BF16)
        v = v_ref[pl.ds(start, tk), :].astype(BF16)
        z = _dot_nt(q, k) * SCALE
        log_stay = _log_sigmoid(-z)
        if diag:
            log_stay = jnp.where(strictly_causal, log_stay, 0.0)
        w = jnp.dot(_split_stack(log_stay), after, preferred_element_type=F32)
        later = w[:tq] + w[tq:] + run
        a = jnp.exp(z + log_stay + later)
        if diag:
            a = jnp.where(strictly_causal, a, 0.0)
        acc = acc + jnp.dot(a.astype(BF16), v, preferred_element_type=F32)
        run = run + jnp.sum(log_stay, axis=-1, keepdims=True)
        return run, acc

    run, acc = block(i, jnp.zeros((tq, 1), F32), jnp.zeros((tq, HEAD_DIM), F32), True)

    def body(t, carry):
        return block(i - 1 - t, carry[0], carry[1], False)

    run, acc = lax.fori_loop(0, i, body, (run, acc))
    o_ref[...] = acc


def _sb_attention(z, batch, seq, tq, tk):
    nq = seq // tq
    return pl.pallas_call(
        functools.partial(_sb_kernel, tq=tq, tk=tk),
        grid=(batch, N_SB, nq),
        in_specs=[
            pl.BlockSpec((None, tq, HEAD_DIM), lambda b, h, i: (0, b * nq + i, h)),
            pl.BlockSpec((None, seq, HEAD_DIM), lambda b, h, i: (1, b, h)),
            pl.BlockSpec((None, seq, HEAD_DIM), lambda b, h, i: (2, b, h)),
        ],
        out_specs=pl.BlockSpec((tq, HEAD_DIM), lambda b, h, i: (b * nq + i, h)),
        out_shape=jax.ShapeDtypeStruct((batch * seq, N_SB * HEAD_DIM), F32),
        compiler_params=_cparams(("arbitrary", "arbitrary", "arbitrary")),
        name="sb_attention",
    )(z, z, z)


def _fox_kernel(q_ref, k_ref, v_ref, c_ref, o_ref, *, tq, tk):
    i = pl.program_id(2)
    q = q_ref[...].astype(BF16)
    causal = lax.broadcasted_iota(jnp.int32, (tq, tk), 1) <= lax.broadcasted_iota(jnp.int32, (tq, tk), 0)
    c_q0 = c_ref[:, pl.ds(pl.multiple_of(i * tq, tq), tq)][:, 0:1]

    def block(j, m, l, acc, diag):
        start = pl.multiple_of(j * tk, tk)
        k = k_ref[pl.ds(start, tk), :].astype(BF16)
        v = v_ref[pl.ds(start, tk), :].astype(BF16)
        bias = c_q0 - c_ref[:, pl.ds(start, tk)]
        s = _dot_nt(q, k) * SCALE + bias
        if diag:
            s = jnp.where(causal, s, NEG_INF)
        m_new = jnp.maximum(m, jnp.max(s, axis=-1, keepdims=True))
        alpha = jnp.exp(m - m_new)
        p = jnp.exp(s - m_new)
        l = alpha * l + jnp.sum(p, axis=-1, keepdims=True)
        acc = alpha * acc + jnp.dot(p.astype(BF16), v, preferred_element_type=F32)
        return m_new, l, acc

    m, l, acc = block(i, jnp.full((tq, 1), NEG_INF, F32), jnp.zeros((tq, 1), F32),
                      jnp.zeros((tq, HEAD_DIM), F32), True)

    def body(t, carry):
        return block(i - 1 - t, carry[0], carry[1], carry[2], False)

    m, l, acc = lax.fori_loop(0, i, body, (m, l, acc))
    o_ref[...] = acc / l


def _fox_attention(z, c_rows, batch, seq, tq, tk):
    nq = seq // tq
    return pl.pallas_call(
        functools.partial(_fox_kernel, tq=tq, tk=tk),
        grid=(batch, N_FOX, nq),
        in_specs=[
            pl.BlockSpec((None, tq, HEAD_DIM), lambda b, h, i: (3, b * nq + i, h)),
            pl.BlockSpec((None, seq, HEAD_DIM), lambda b, h, i: (4, b, h)),
            pl.BlockSpec((None, seq, HEAD_DIM), lambda b, h, i: (5, b, h)),
            pl.BlockSpec((None, 1, seq), lambda b, h, i: (b * N_FOX + h, 0, 0)),
        ],
        out_specs=pl.BlockSpec((tq, HEAD_DIM), lambda b, h, i: (b * nq + i, h)),
        out_shape=jax.ShapeDtypeStruct((batch * seq, N_FOX * HEAD_DIM), F32),
        compiler_params=_cparams(("arbitrary", "arbitrary", "arbitrary")),
        name="fox_attention",
    )(z, z, z, c_rows)


def _reverse_cumsum_lanes(x, lane):
    n = x.shape[-1]
    d = N_SB
    while d < n:
        shifted = pltpu.roll(x, n - d, 1)
        x = x + jnp.where(lane + d < n, shifted, 0.0)
        d *= 2
    return x


def _decode_kernel(pt_ref, qs_ref, qf_ref, kown_ref, vown_ref, lfown_ref, *refs, pages_per_step):
    del pt_ref
    pp = pages_per_step
    ksb = refs[0:pp]
    vsb = refs[pp:2 * pp]
    kfx = refs[2 * pp:3 * pp]
    vfx = refs[3 * pp:4 * pp]
    lfc = refs[4 * pp:5 * pp]
    osb_ref, ofx_ref = refs[5 * pp:5 * pp + 2]
    acc_sb, acc_fx, run_sb, run_fx, m_ref, l_ref = refs[5 * pp + 2:]
    j = pl.program_id(1)
    rows = ksb[0].shape[0] * ksb[0].shape[1]

    qs = qs_ref[...]
    qf = qf_ref[...]

    @pl.when(j == 0)
    def _():
        acc_sb[...] = jnp.zeros_like(acc_sb)
        run_sb[...] = jnp.zeros_like(run_sb)
        s_own = jnp.sum(qf * kown_ref[...], axis=-1, keepdims=True) * SCALE
        m_ref[...] = s_own
        l_ref[...] = jnp.ones_like(l_ref)
        acc_fx[...] = vown_ref[...]
        run_fx[...] = lfown_ref[...]

    lane = lax.broadcasted_iota(jnp.int32, (N_SB, rows), 1)
    head = lax.broadcasted_iota(jnp.int32, (N_SB, rows), 0)
    valid = (lane & (N_SB - 1)) == head
    qs_b = qs.astype(BF16)
    qf_b = qf.astype(BF16)

    for p in range(pp):
        k = ksb[p][...].reshape(rows, HEAD_DIM).astype(BF16)
        v = vsb[p][...].reshape(rows, HEAD_DIM).astype(BF16)
        z = _dot_nt(qs_b, k) * SCALE
        log_stay = jnp.where(valid, _log_sigmoid(-z), 0.0)
        incl = _reverse_cumsum_lanes(log_stay, lane)
        run = run_sb[...]
        a = jnp.where(valid, jnp.exp(z + incl + run), 0.0)
        acc_sb[...] += jnp.dot(a.astype(BF16), v, preferred_element_type=F32)
        run_sb[...] = run + jnp.sum(log_stay, axis=-1, keepdims=True)

        k = kfx[p][...].reshape(rows, HEAD_DIM).astype(BF16)
        v = vfx[p][...].reshape(rows, HEAD_DIM).astype(BF16)
        lf = jnp.where(valid, lfc[p][...], 0.0)
        incl = _reverse_cumsum_lanes(lf, lane)
        run = run_fx[...]
        s = jnp.where(valid, _dot_nt(qf_b, k) * SCALE + (incl - lf) + run, NEG_INF)
        m_old = m_ref[...]
        m_new = jnp.maximum(m_old, jnp.max(s, axis=-1, keepdims=True))
        alpha = jnp.exp(m_old - m_new)
        pr = jnp.exp(s - m_new)
        l_ref[...] = alpha * l_ref[...] + jnp.sum(pr, axis=-1, keepdims=True)
        acc_fx[...] = alpha * acc_fx[...] + jnp.dot(pr.astype(BF16), v, preferred_element_type=F32)
        m_ref[...] = m_new
        run_fx[...] = run + jnp.sum(lf, axis=-1, keepdims=True)

    @pl.when(j == pl.num_programs(1) - 1)
    def _():
        osb_ref[...] = acc_sb[...]
        ofx_ref[...] = acc_fx[...] / l_ref[...]


def _decode_attention(page_table, q_sb, q_fx, k_own, v_own, lf_own, c_sb_k, c_sb_v, c_fx_k, c_fx_v,
                      c_logf_rows, pages_per_step):
    nb, n_pages = page_table.shape
    page, heads, hd = c_sb_k.shape[1:]
    rows = page * heads
    pp = pages_per_step
    steps = n_pages // pp

    def page_of(p):
        return lambda b, j, pt: (pt[b * n_pages + (n_pages - 1 - (j * pp + p))], 0, 0, 0)

    def lf_page_of(p):
        return lambda b, j, pt: (pt[b * n_pages + (n_pages - 1 - (j * pp + p))], 0, 0)

    per_b = pl.BlockSpec((None, heads, hd), lambda b, j, pt: (b, 0, 0))
    cache_specs = [pl.BlockSpec((None, page, heads, hd), page_of(p)) for p in range(pp)]
    lf_specs = [pl.BlockSpec((None, 1, rows), lf_page_of(p)) for p in range(pp)]
    grid_spec = pltpu.PrefetchScalarGridSpec(
        num_scalar_prefetch=1,
        grid=(nb, steps),
        in_specs=[per_b, per_b, per_b, per_b, pl.BlockSpec((None, heads, 1), lambda b, j, pt: (b, 0, 0))]
        + cache_specs * 4 + lf_specs,
        out_specs=[per_b, per_b],
        scratch_shapes=[pltpu.VMEM((heads, hd), F32), pltpu.VMEM((heads, hd), F32),
                        pltpu.VMEM((heads, 1), F32), pltpu.VMEM((heads, 1), F32),
                        pltpu.VMEM((heads, 1), F32), pltpu.VMEM((heads, 1), F32)],
    )
    return pl.pallas_call(
        functools.partial(_decode_kernel, pages_per_step=pp),
        grid_spec=grid_spec,
        out_shape=[jax.ShapeDtypeStruct((nb, heads, hd), F32)] * 2,
        compiler_params=_cparams(("arbitrary", "arbitrary")),
        name="decode_attention",
    )(page_table.reshape(-1), q_sb, q_fx, k_own, v_own, lf_own,
      *([c_sb_k] * pp), *([c_sb_v] * pp), *([c_fx_k] * pp), *([c_fx_v] * pp), *([c_logf_rows] * pp))


def _merge_kernel(osb_ref, ofx_ref, x_ref, gsb_ref, gfx_ref, wout_ref, gffn_ref, wr_ref, br_ref,
                  hp_ref, u_ref, comb_ref):
    o = jnp.concatenate([_rms(osb_ref[...], gsb_ref[...]), _rms(ofx_ref[...], gfx_ref[...])], axis=-1)
    y = jnp.dot(o.astype(BF16), wout_ref[...], preferred_element_type=F32)
    hp = x_ref[...] + y
    hp_ref[...] = hp
    u = _rms(hp, gffn_ref[...])
    u_ref[...] = u.astype(BF16)

    tm = u.shape[0]
    parts = jnp.dot(_split_stack(u), wr_ref[...], preferred_element_type=F32)
    both = parts[:tm] + parts[tm:]
    logits = both[:, :LANES] + both[:, LANES:] + br_ref[...]

    lane = lax.broadcasted_iota(jnp.int32, (tm, LANES), 1).astype(F32)
    is_group = (lane >= N_EXPERTS) & (lane < N_EXPERTS + N_GROUPS)
    big = float(1 << 20)
    lg = jnp.where(is_group, logits, NEG_INF)
    lg_max = jnp.max(lg, axis=-1, keepdims=True)
    gsel = jnp.min(jnp.where(lg == lg_max, lane, big), axis=-1, keepdims=True) - N_EXPERTS
    g1 = 1.0 / jnp.sum(jnp.exp(lg - lg_max), axis=-1, keepdims=True)

    in_group = (lane >= gsel * EXPERTS_PER_GROUP) & (lane < (gsel + 1) * EXPERTS_PER_GROUP)
    le = jnp.where(in_group, logits, NEG_INF)
    v_a = jnp.max(le, axis=-1, keepdims=True)
    i_a = jnp.min(jnp.where(le == v_a, lane, big), axis=-1, keepdims=True)
    le_b = jnp.where(lane == i_a, NEG_INF, le)
    v_b = jnp.max(le_b, axis=-1, keepdims=True)
    i_b = jnp.min(jnp.where(le_b == v_b, lane, big), axis=-1, keepdims=True)
    e_b = jnp.exp(v_b - v_a)
    gate_a = g1 / (1.0 + e_b)
    gate_b = g1 * e_b / (1.0 + e_b)
    comb_ref[...] = jnp.where(lane == i_a, gate_a, 0.0) + jnp.where(lane == i_b, gate_b, 0.0)


def _merge(o_sb, o_fx, x, g_sb, g_fx, w_out, g_ffn, w_r, b_r, tm):
    t, d = x.shape
    row = lambda n: pl.BlockSpec((tm, n), lambda i: (i, 0))
    const = lambda a: pl.BlockSpec(a.shape, lambda i: (0,) * a.ndim)
    return pl.pallas_call(
        _merge_kernel,
        grid=(t // tm,),
        in_specs=[row(o_sb.shape[1]), row(o_fx.shape[1]), row(d), const(g_sb), const(g_fx), const(w_out),
                  const(g_ffn), const(w_r), const(b_r)],
        out_specs=[row(d), row(d), row(LANES)],
        out_shape=[jax.ShapeDtypeStruct((t, d), F32), jax.ShapeDtypeStruct((t, d), BF16),
                   jax.ShapeDtypeStruct((t, LANES), F32)],
        compiler_params=_cparams(("arbitrary",)),
        name="merge_router",
    )(o_sb, o_fx, x, g_sb, g_fx, w_out, g_ffn, w_r, b_r)


def _moe_kernel(u_ref, comb_ref, hp_ref, wg_ref, wu_ref, wd_ref, gfin_ref, y_ref, acc_ref):
    e = pl.program_id(1)

    @pl.when(e == 0)
    def _():
        acc_ref[...] = jnp.zeros_like(acc_ref)

    u = u_ref[...]
    gate = jnp.dot(u, wg_ref[...], preferred_element_type=F32)
    up = jnp.dot(u, wu_ref[...], preferred_element_type=F32)
    lane = lax.broadcasted_iota(jnp.int32, comb_ref.shape, 1)
    weight = jnp.sum(jnp.where(lane == e, comb_ref[...], 0.0), axis=-1, keepdims=True)
    h = gate * (1.0 / (1.0 + jnp.exp(-gate))) * up * weight
    acc_ref[...] += jnp.dot(h.astype(BF16), wd_ref[...], preferred_element_type=F32)

    @pl.when(e == pl.num_programs(1) - 1)
    def _():
        y_ref[...] = _rms(hp_ref[...] + acc_ref[...], gfin_ref[...])


def _moe(u, comb, hp, w_gate, w_up, w_down, g_final, tm):
    t, d = hp.shape
    n_e, _, d_e = w_gate.shape
    return pl.pallas_call(
        _moe_kernel,
        grid=(t // tm, n_e),
        in_specs=[
            pl.BlockSpec((tm, d), lambda i, e: (i, 0)),
            pl.BlockSpec((tm, LANES), lambda i, e: (i, 0)),
            pl.BlockSpec((tm, d), lambda i, e: (i, 0)),
            pl.BlockSpec((None, d, d_e), lambda i, e: (e, 0, 0)),
            pl.BlockSpec((None, d, d_e), lambda i, e: (e, 0, 0)),
            pl.BlockSpec((None, d_e, d), lambda i, e: (e, 0, 0)),
            pl.BlockSpec((1, d), lambda i, e: (0, 0)),
        ],
        out_specs=pl.BlockSpec((tm, d), lambda i, e: (i, 0)),
        out_shape=jax.ShapeDtypeStruct((t, d), F32),
        scratch_shapes=[pltpu.VMEM((tm, d), F32)],
        compiler_params=_cparams(("arbitrary", "arbitrary")),
        name="moe_experts",
    )(u, comb, hp, w_gate, w_up, w_down, g_final)


def _pad_lanes(a, n=LANES):
    return jnp.pad(a, ((0, 0), (0, n - a.shape[1])))


def kernel(x_prompt, x_sample, cache_sb_k, cache_sb_v, cache_fox_k, cache_fox_v, cache_fox_logf, page_table, norm_attn_g, w_in, b_forget, g_sb_out, g_fox_out, w_out, norm_ffn_g, w_router_group, b_router_group, w_router_expert, b_router_expert, w_expert_gate, w_expert_up, w_expert_down, norm_final_g):
    batch, seq, d = x_prompt.shape
    nb = x_sample.shape[0]
    assert w_in.shape[0] == 1, "one trunk layer"
    d_qkv = 3 * (N_SB + N_FOX) * HEAD_DIM

    w_main = w_in[0, :, :d_qkv].astype(BF16)
    w_f = _pad_lanes(w_in[0, :, d_qkv:]).astype(BF16)
    b_f = _pad_lanes(b_forget)
    g_attn = norm_attn_g
    w_out_b = w_out[0].astype(BF16)
    w_r = jnp.concatenate([w_router_expert[0].transpose(1, 0, 2).reshape(d, N_EXPERTS), w_router_group[0]], axis=1)
    w_r = _pad_lanes(w_r)
    w_r_hi = w_r.astype(BF16)
    w_r_lo = (w_r - w_r_hi.astype(F32)).astype(BF16)
    w_r2 = jnp.concatenate([w_r_hi, w_r_lo], axis=1)
    b_r = _pad_lanes(jnp.concatenate([b_router_expert[0].reshape(1, N_EXPERTS), b_router_group], axis=1))
    wg_b = w_expert_gate[0].astype(BF16)
    wu_b = w_expert_up[0].astype(BF16)
    wd_b = w_expert_down[0].astype(BF16)
    g_fin = norm_final_g.reshape(1, d)

    xp = x_prompt.reshape(batch * seq, d)
    z_p, logf_p = _inproj(xp, g_attn, w_main, w_f, b_f, tm=512)
    logf_p = logf_p[:, :N_FOX].reshape(batch, seq, N_FOX)
    c_rows = _cumsum_rows(logf_p.transpose(0, 2, 1)).reshape(batch * N_FOX, 1, seq)
    o_sb = _sb_attention(z_p, batch, seq, tq=256, tk=256)
    o_fx = _fox_attention(z_p, c_rows, batch, seq, tq=256, tk=256)
    hp_p, u_p, comb_p = _merge(o_sb, o_fx, xp, g_sb_out, g_fox_out, w_out_b, norm_ffn_g, w_r2, b_r, tm=256)
    y_prompt = _moe(u_p, comb_p, hp_p, wg_b, wu_b, wd_b, g_fin, tm=512).reshape(batch, seq, d)

    xs = x_sample.reshape(nb, d)
    z_s, logf_s = _inproj(xs, g_attn, w_main, w_f, b_f, tm=nb)
    logf_s = logf_s[:, :N_FOX]
    per_head = lambda a: a.reshape(nb, N_SB, HEAD_DIM)
    page = cache_fox_logf.shape[2]
    logf_rows = cache_fox_logf[0].reshape(-1, 1, page * N_FOX)
    o_sb_s, o_fx_s = _decode_attention(
        page_table, per_head(z_s[0]), per_head(z_s[3]), per_head(z_s[4]), per_head(z_s[5]),
        logf_s.reshape(nb, N_FOX, 1), cache_sb_k[0], cache_sb_v[0], cache_fox_k[0], cache_fox_v[0],
        logf_rows, pages_per_step=4)
    hp_s, u_s, comb_s = _merge(o_sb_s.reshape(nb, -1), o_fx_s.reshape(nb, -1), xs, g_sb_out, g_fox_out,
                               w_out_b, norm_ffn_g, w_r2, b_r, tm=nb)
    y_sample = _moe(u_s, comb_s, hp_s, wg_b, wu_b, wd_b, g_fin, tm=nb).reshape(nb, 1, d)

    heads_p = lambda a: a.reshape(1, batch, seq, N_SB, HEAD_DIM)
    heads_s = lambda a: a.reshape(1, nb, 1, N_SB, HEAD_DIM)
    return (y_prompt, y_sample,
            heads_p(z_p[1]), heads_p(z_p[2]), heads_p(z_p[4]), heads_p(z_p[5]), logf_p[None],
            heads_s(z_s[1]), heads_s(z_s[2]), heads_s(z_s[4]), heads_s(z_s[5]),
            logf_s.reshape(1, nb, 1, N_FOX))
```

```python
import functools

import jax
import jax.numpy as jnp
from jax import lax
from jax.experimental import pallas as pl
from jax.experimental.pallas import tpu as pltpu

HEAD_DIM = 128
N_SB = 8
N_FOX = 8
N_GROUPS = 4
EXPERTS_PER_GROUP = 8
N_EXPERTS = N_GROUPS * EXPERTS_PER_GROUP
TOP_K = 2
SCALE = HEAD_DIM ** -0.5
LOG2E = 1.4426950408889634
RMS_EPS = 1e-6
LANES = 128
VMEM_LIMIT = 56 * 1024 * 1024
NEG_INF = float("-inf")
BF16 = jnp.bfloat16
F32 = jnp.float32
I32 = jnp.int32

TM_INPROJ = 512
TQ = 256
TK = 256
HEADS_PER_STEP = 8
PAGES_PER_STEP = 4
TM_MERGE = 256
TILE_EXPERT = 256
TM_COMBINE = 256


def _cparams(sem):
    return pltpu.CompilerParams(dimension_semantics=sem, vmem_limit_bytes=VMEM_LIMIT)


def _rms(x, g):
    return x * lax.rsqrt(jnp.mean(x * x, axis=-1, keepdims=True) + RMS_EPS) * g


def _log_sigmoid(x):
    return jnp.minimum(x, 0.0) - jnp.log(1.0 + jnp.exp(-jnp.abs(x)))


def _split_terms(x, terms=2):
    pieces = []
    rest = x
    for _ in range(terms - 1):
        head = rest.astype(BF16).astype(F32)
        pieces.append(head)
        rest = rest - head
    pieces.append(rest)
    return pieces


def _log2_sigmoid_neg(z2):
    neg_abs = pltpu.bitcast(pltpu.bitcast(z2, jnp.uint32) | jnp.uint32(0x80000000), F32)
    return jnp.log(1.0 + jnp.exp2(neg_abs)) * (-LOG2E) - jnp.maximum(z2, 0.0)


def _split_hi_lo_lanes(x):
    hi = pltpu.bitcast(pltpu.bitcast(x, jnp.uint32) & jnp.uint32(0xFFFF0000), F32)
    return jnp.concatenate([hi, x - hi], axis=1).astype(BF16)


def _dot_nt(a, b):
    return lax.dot_general(a, b, (((1,), (1,)), ((), ())), preferred_element_type=F32)


def _inproj_kernel(x_ref, g_ref, w_ref, wf_ref, bf_ref, z_ref, logf_ref, u_ref):
    j = pl.program_id(1)

    @pl.when(j == 0)
    def _():
        u = _rms(x_ref[...], g_ref[...]).astype(BF16)
        u_ref[...] = u
        f = jnp.dot(u, wf_ref[...], preferred_element_type=F32) + bf_ref[...]
        logf_ref[...] = _log_sigmoid(f)

    z_ref[...] = jnp.dot(u_ref[...], w_ref[...], preferred_element_type=F32)


def _inproj(x, g, w_main, w_f, b_f, tm):
    t, d = x.shape
    n_out = w_main.shape[1] // 1024
    return pl.pallas_call(
        _inproj_kernel,
        grid=(t // tm, n_out),
        in_specs=[
            pl.BlockSpec((tm, d), lambda i, j: (i, 0)),
            pl.BlockSpec((1, d), lambda i, j: (0, 0)),
            pl.BlockSpec((d, 1024), lambda i, j: (0, j)),
            pl.BlockSpec((d, LANES), lambda i, j: (0, 0)),
            pl.BlockSpec((1, LANES), lambda i, j: (0, 0)),
        ],
        out_specs=[
            pl.BlockSpec((None, tm, 1024), lambda i, j: (j, i, 0)),
            pl.BlockSpec((tm, LANES), lambda i, j: (i, 0)),
        ],
        out_shape=[
            jax.ShapeDtypeStruct((n_out, t, 1024), F32),
            jax.ShapeDtypeStruct((t, LANES), F32),
        ],
        scratch_shapes=[pltpu.VMEM((tm, d), BF16)],
        compiler_params=_cparams(("arbitrary", "arbitrary")),
        name="inproj",
    )(x, g, w_main, w_f, b_f)


def _cumsum_kernel(x_ref, c_ref, *, chunk):
    s = x_ref.shape[-1]
    r = lax.broadcasted_iota(I32, (chunk, chunk), 0)
    c = lax.broadcasted_iota(I32, (chunk, chunk), 1)
    upper = (r <= c).astype(BF16)
    carry = jnp.zeros((x_ref.shape[0], 1), F32)
    for k in range(s // chunk):
        x = x_ref[:, k * chunk:(k + 1) * chunk]
        stacked = jnp.concatenate(_split_terms(x, 3), axis=0).astype(BF16)
        parts = jnp.dot(stacked, upper, preferred_element_type=F32)
        n = x.shape[0]
        cs = parts[:n] + parts[n:2 * n] + parts[2 * n:] + carry
        c_ref[:, k * chunk:(k + 1) * chunk] = cs
        carry = cs[:, chunk - 1:chunk]


def _cumsum_rows(x):
    b, h, s = x.shape
    return pl.pallas_call(
        functools.partial(_cumsum_kernel, chunk=256),
        grid=(b,),
        in_specs=[pl.BlockSpec((None, h, s), lambda i: (i, 0, 0))],
        out_specs=pl.BlockSpec((None, h, s), lambda i: (i, 0, 0)),
        out_shape=jax.ShapeDtypeStruct((b, h, s), F32),
        compiler_params=_cparams(("arbitrary",)),
        name="logf_cumsum",
    )(x)


def _sb_kernel(q_ref, k_ref, v_ref, o_ref, *, tq, tk, heads):
    i = pl.program_id(2)
    r = lax.broadcasted_iota(I32, (2 * tk, tk), 0)
    c = lax.broadcasted_iota(I32, (2 * tk, tk), 1)
    not_before2 = ((r & (tk - 1)) >= c).astype(BF16)
    strictly_causal = lax.broadcasted_iota(I32, (tq, tk), 1) < lax.broadcasted_iota(I32, (tq, tk), 0)
    cols = [slice(g * HEAD_DIM, (g + 1) * HEAD_DIM) for g in range(heads)]
    q = [(q_ref[:, cs] * (SCALE * LOG2E)).astype(BF16) for cs in cols]
    hs = range(heads)

    def block(j, carry, diag):
        start = pl.multiple_of(j * tk, tk)
        z = [_dot_nt(q[g], k_ref[pl.ds(start, tk), cols[g]].astype(BF16)) for g in hs]
        log_stay = [_log2_sigmoid_neg(z[g]) for g in hs]
        if diag:
            log_stay = [jnp.where(strictly_causal, x, 0.0) for x in log_stay]
        incl = [jnp.dot(_split_hi_lo_lanes(log_stay[g]), not_before2, preferred_element_type=F32) for g in hs]
        a = [jnp.exp2(z[g] + incl[g] + carry[g][0]) for g in hs]
        if diag:
            a = [jnp.where(strictly_causal, x, 0.0) for x in a]
        pv = [jnp.dot(a[g].astype(BF16), v_ref[pl.ds(start, tk), cols[g]].astype(BF16),
                      preferred_element_type=F32) for g in hs]
        return tuple((carry[g][0] + incl[g][:, 0:1], carry[g][1] + pv[g]) for g in hs)

    init = tuple((jnp.zeros((tq, 1), F32), jnp.zeros((tq, HEAD_DIM), F32)) for _ in hs)
    carry = block(i, init, True)
    carry = lax.fori_loop(0, i, lambda t, cy: block(i - 1 - t, cy, False), carry)
    for g in range(heads):
        o_ref[:, cols[g]] = carry[g][1]


def _sb_attention(z, batch, seq, tq, tk, heads):
    nq = seq // tq
    width = heads * HEAD_DIM
    return pl.pallas_call(
        functools.partial(_sb_kernel, tq=tq, tk=tk, heads=heads),
        grid=(batch, N_SB // heads, nq),
        in_specs=[
            pl.BlockSpec((None, tq, width), lambda b, h, i: (0, b * nq + i, h)),
            pl.BlockSpec((None, seq, width), lambda b, h, i: (1, b, h)),
            pl.BlockSpec((None, seq, width), lambda b, h, i: (2, b, h)),
        ],
        out_specs=pl.BlockSpec((tq, width), lambda b, h, i: (b * nq + i, h)),
        out_shape=jax.ShapeDtypeStruct((batch * seq, N_SB * HEAD_DIM), F32),
        compiler_params=_cparams(("arbitrary", "arbitrary", "arbitrary")),
        name="sb_attention",
    )(z, z, z)


def _fox_kernel(q_ref, k_ref, v_ref, c_ref, o_ref, *, tq, tk, heads):
    i = pl.program_id(2)
    causal = lax.broadcasted_iota(I32, (tq, tk), 1) <= lax.broadcasted_iota(I32, (tq, tk), 0)
    cols = [slice(g * HEAD_DIM, (g + 1) * HEAD_DIM) for g in range(heads)]
    q = [(q_ref[:, cs] * (SCALE * LOG2E)).astype(BF16) for cs in cols]
    q_start = pl.multiple_of(i * tq, tq)
    c_q0 = [c_ref[g, :, pl.ds(q_start, tq)][:, 0:1] for g in range(heads)]
    hs = range(heads)

    def block(j, carry, diag):
        start = pl.multiple_of(j * tk, tk)
        s = [_dot_nt(q[g], k_ref[pl.ds(start, tk), cols[g]].astype(BF16))
             + (c_q0[g] - c_ref[g, :, pl.ds(start, tk)]) * LOG2E for g in hs]
        if diag:
            s = [jnp.where(causal, x, NEG_INF) for x in s]
        m_new = [jnp.maximum(carry[g][0], jnp.max(s[g], axis=-1, keepdims=True)) for g in hs]
        p = [jnp.exp2(s[g] - m_new[g]) for g in hs]
        alpha = [jnp.exp2(carry[g][0] - m_new[g]) for g in hs]
        pv = [jnp.dot(p[g].astype(BF16), v_ref[pl.ds(start, tk), cols[g]].astype(BF16),
                      preferred_element_type=F32) for g in hs]
        return tuple((m_new[g], alpha[g] * carry[g][1] + jnp.sum(p[g], axis=-1, keepdims=True),
                      alpha[g] * carry[g][2] + pv[g]) for g in hs)

    init = tuple((jnp.full((tq, 1), NEG_INF, F32), jnp.zeros((tq, 1), F32), jnp.zeros((tq, HEAD_DIM), F32))
                 for _ in hs)
    carry = block(i, init, True)
    carry = lax.fori_loop(0, i, lambda t, cy: block(i - 1 - t, cy, False), carry)
    for g in range(heads):
        o_ref[:, cols[g]] = carry[g][2] / carry[g][1]


def _fox_attention(z, c_rows, batch, seq, tq, tk, heads):
    nq = seq // tq
    width = heads * HEAD_DIM
    groups = N_FOX // heads
    return pl.pallas_call(
        functools.partial(_fox_kernel, tq=tq, tk=tk, heads=heads),
        grid=(batch, groups, nq),
        in_specs=[
            pl.BlockSpec((None, tq, width), lambda b, h, i: (3, b * nq + i, h)),
            pl.BlockSpec((None, seq, width), lambda b, h, i: (4, b, h)),
            pl.BlockSpec((None, seq, width), lambda b, h, i: (5, b, h)),
            pl.BlockSpec((heads, 1, seq), lambda b, h, i: (b * groups + h, 0, 0)),
        ],
        out_specs=pl.BlockSpec((tq, width), lambda b, h, i: (b * nq + i, h)),
        out_shape=jax.ShapeDtypeStruct((batch * seq, N_FOX * HEAD_DIM), F32),
        compiler_params=_cparams(("arbitrary", "arbitrary", "arbitrary")),
        name="fox_attention",
    )(z, z, z, c_rows)


def _reverse_cumsum_lanes(x, lane):
    n = x.shape[-1]
    d = N_SB
    while d < n:
        shifted = pltpu.roll(x, n - d, 1)
        x = x + jnp.where(lane + d < n, shifted, 0.0)
        d *= 2
    return x


def _decode_kernel(pt_ref, qs_ref, qf_ref, kown_ref, vown_ref, lfown_ref, *refs, pages_per_step):
    del pt_ref
    pp = pages_per_step
    ksb = refs[0:pp]
    vsb = refs[pp:2 * pp]
    kfx = refs[2 * pp:3 * pp]
    vfx = refs[3 * pp:4 * pp]
    lfc = refs[4 * pp:5 * pp]
    osb_ref, ofx_ref = refs[5 * pp:5 * pp + 2]
    acc_sb, acc_fx, run_sb, run_fx, m_ref, l_ref = refs[5 * pp + 2:]
    j = pl.program_id(1)
    page_rows = ksb[0].shape[0] * ksb[0].shape[1]
    rows = pp * page_rows

    qs = qs_ref[...] * SCALE
    qf = qf_ref[...] * SCALE

    @pl.when(j == 0)
    def _():
        acc_sb[...] = jnp.zeros_like(acc_sb)
        run_sb[...] = jnp.zeros_like(run_sb)
        m_ref[...] = jnp.sum(qf * kown_ref[...], axis=-1, keepdims=True)
        l_ref[...] = jnp.ones_like(l_ref)
        acc_fx[...] = vown_ref[...]
        run_fx[...] = lfown_ref[...]

    def rows_of(page_refs):
        return jnp.concatenate([page_refs[p][...].reshape(page_rows, HEAD_DIM).astype(BF16)
                                for p in reversed(range(pp))], axis=0)

    lane = lax.broadcasted_iota(I32, (N_SB, rows), 1)
    head = lax.broadcasted_iota(I32, (N_SB, rows), 0)
    valid = (lane & (N_SB - 1)) == head

    z = _dot_nt(qs.astype(BF16), rows_of(ksb))
    log_stay = jnp.where(valid, _log_sigmoid(-z), 0.0)
    incl = _reverse_cumsum_lanes(log_stay, lane)
    run = run_sb[...]
    a = jnp.where(valid, jnp.exp(z + incl + run), 0.0)
    acc_sb[...] += jnp.dot(a.astype(BF16), rows_of(vsb), preferred_element_type=F32)
    run_sb[...] = run + jnp.sum(log_stay, axis=-1, keepdims=True)

    lf = jnp.where(valid, jnp.concatenate([lfc[p][...] for p in reversed(range(pp))], axis=1), 0.0)
    incl = _reverse_cumsum_lanes(lf, lane)
    run = run_fx[...]
    s = jnp.where(valid, _dot_nt(qf.astype(BF16), rows_of(kfx)) + (incl - lf) + run, NEG_INF)
    m_old = m_ref[...]
    m_new = jnp.maximum(m_old, jnp.max(s, axis=-1, keepdims=True))
    alpha = jnp.exp(m_old - m_new)
    pr = jnp.exp(s - m_new)
    l_ref[...] = alpha * l_ref[...] + jnp.sum(pr, axis=-1, keepdims=True)
    acc_fx[...] = alpha * acc_fx[...] + jnp.dot(pr.astype(BF16), rows_of(vfx), preferred_element_type=F32)
    m_ref[...] = m_new
    run_fx[...] = run + jnp.sum(lf, axis=-1, keepdims=True)

    @pl.when(j == pl.num_programs(1) - 1)
    def _():
        osb_ref[...] = acc_sb[...]
        ofx_ref[...] = acc_fx[...] / l_ref[...]


def _decode_attention(page_table, q_sb, q_fx, k_own, v_own, lf_own, c_sb_k, c_sb_v, c_fx_k, c_fx_v,
                      c_logf_rows, pages_per_step):
    nb, n_pages = page_table.shape
    page, heads, hd = c_sb_k.shape[1:]
    rows = page * heads
    pp = pages_per_step
    steps = n_pages // pp

    def page_of(p):
        return lambda b, j, pt: (pt[b * n_pages + (n_pages - 1 - (j * pp + p))], 0, 0, 0)

    def lf_page_of(p):
        return lambda b, j, pt: (pt[b * n_pages + (n_pages - 1 - (j * pp + p))], 0, 0)

    per_b = pl.BlockSpec((None, heads, hd), lambda b, j, pt: (b, 0, 0))
    cache_specs = [pl.BlockSpec((None, page, heads, hd), page_of(p)) for p in range(pp)]
    lf_specs = [pl.BlockSpec((None, 1, rows), lf_page_of(p)) for p in range(pp)]
    grid_spec = pltpu.PrefetchScalarGridSpec(
        num_scalar_prefetch=1,
        grid=(nb, steps),
        in_specs=[per_b, per_b, per_b, per_b, pl.BlockSpec((None, heads, 1), lambda b, j, pt: (b, 0, 0))]
        + cache_specs * 4 + lf_specs,
        out_specs=[per_b, per_b],
        scratch_shapes=[pltpu.VMEM((heads, hd), F32), pltpu.VMEM((heads, hd), F32),
                        pltpu.VMEM((heads, 1), F32), pltpu.VMEM((heads, 1), F32),
                        pltpu.VMEM((heads, 1), F32), pltpu.VMEM((heads, 1), F32)],
    )
    return pl.pallas_call(
        functools.partial(_decode_kernel, pages_per_step=pp),
        grid_spec=grid_spec,
        out_shape=[jax.ShapeDtypeStruct((nb, heads, hd), F32)] * 2,
        compiler_params=_cparams(("arbitrary", "arbitrary")),
        name="decode_attention",
    )(page_table.reshape(-1), q_sb, q_fx, k_own, v_own, lf_own,
      *([c_sb_k] * pp), *([c_sb_v] * pp), *([c_fx_k] * pp), *([c_fx_v] * pp), *([c_logf_rows] * pp))


ROUTE_EXPERT, ROUTE_GATE, ROUTE_RANK = 0, 2, 4


def _merge_kernel(osb_ref, ofx_ref, x_ref, gsb_ref, gfx_ref, wout_ref, gffn_ref, wr_ref, br_ref,
                  hp_ref, u_ref, route_ref, count_ref, base_ref):
    @pl.when(pl.program_id(0) == 0)
    def _():
        base_ref[...] = jnp.zeros_like(base_ref)

    o = jnp.concatenate([_rms(osb_ref[...], gsb_ref[...]), _rms(ofx_ref[...], gfx_ref[...])], axis=-1)
    y = jnp.dot(o.astype(BF16), wout_ref[...], preferred_element_type=F32)
    hp = x_ref[...] + y
    hp_ref[...] = hp
    u = _rms(hp, gffn_ref[...])
    u_ref[...] = u

    tm = u.shape[0]
    stacked = jnp.concatenate(_split_terms(u), axis=0).astype(BF16)
    parts = jnp.dot(stacked, wr_ref[...], preferred_element_type=F32)
    both = parts[:tm] + parts[tm:]
    logits = both[:, :LANES] + both[:, LANES:] + br_ref[...]

    lane = lax.broadcasted_iota(I32, (tm, LANES), 1).astype(F32)
    is_group = (lane >= N_EXPERTS) & (lane < N_EXPERTS + N_GROUPS)
    big = float(1 << 20)
    lg = jnp.where(is_group, logits, NEG_INF)
    lg_max = jnp.max(lg, axis=-1, keepdims=True)
    gsel = jnp.min(jnp.where(lg == lg_max, lane, big), axis=-1, keepdims=True) - N_EXPERTS
    g1 = 1.0 / jnp.sum(jnp.exp(lg - lg_max), axis=-1, keepdims=True)

    in_group = (lane >= gsel * EXPERTS_PER_GROUP) & (lane < (gsel + 1) * EXPERTS_PER_GROUP)
    le = jnp.where(in_group, logits, NEG_INF)
    v_a = jnp.max(le, axis=-1, keepdims=True)
    i_a = jnp.min(jnp.where(le == v_a, lane, big), axis=-1, keepdims=True)
    le_b = jnp.where(lane == i_a, NEG_INF, le)
    v_b = jnp.max(le_b, axis=-1, keepdims=True)
    i_b = jnp.min(jnp.where(le_b == v_b, lane, big), axis=-1, keepdims=True)
    e_b = jnp.exp(v_b - v_a)
    gate_a = g1 / (1.0 + e_b)
    gate_b = g1 * e_b / (1.0 + e_b)

    onehot = jnp.where((lane == i_a) | (lane == i_b), 1.0, 0.0)
    rr = lax.broadcasted_iota(I32, (tm, tm), 0)
    cc = lax.broadcasted_iota(I32, (tm, tm), 1)
    count_dtype = BF16 if tm % 16 == 0 else F32
    earlier = (cc < rr).astype(count_dtype)
    rank_all = jnp.dot(earlier, onehot.astype(count_dtype), preferred_element_type=F32) + base_ref[...]
    rank_a = jnp.sum(jnp.where(lane == i_a, rank_all, 0.0), axis=-1, keepdims=True)
    rank_b = jnp.sum(jnp.where(lane == i_b, rank_all, 0.0), axis=-1, keepdims=True)
    total = base_ref[...] + jnp.sum(onehot, axis=0, keepdims=True)
    base_ref[...] = total
    count_ref[...] = total

    route = jnp.zeros((tm, LANES), F32)
    for k, val in ((ROUTE_EXPERT, i_a), (ROUTE_EXPERT + 1, i_b), (ROUTE_GATE, gate_a), (ROUTE_GATE + 1, gate_b),
                   (ROUTE_RANK, rank_a), (ROUTE_RANK + 1, rank_b)):
        route = jnp.where(lane == k, val, route)
    route_ref[...] = route


def _merge(o_sb, o_fx, x, g_sb, g_fx, w_out, g_ffn, w_r, b_r, tm):
    t, d = x.shape
    row = lambda n: pl.BlockSpec((tm, n), lambda i: (i, 0))
    const = lambda a: pl.BlockSpec(a.shape, lambda i: (0,) * a.ndim)
    return pl.pallas_call(
        _merge_kernel,
        grid=(t // tm,),
        in_specs=[row(o_sb.shape[1]), row(o_fx.shape[1]), row(d), const(g_sb), const(g_fx), const(w_out),
                  const(g_ffn), const(w_r), const(b_r)],
        out_specs=[row(d), row(d), row(LANES), pl.BlockSpec((1, LANES), lambda i: (0, 0))],
        out_shape=[jax.ShapeDtypeStruct((t, d), F32), jax.ShapeDtypeStruct((t, d), F32),
                   jax.ShapeDtypeStruct((t, LANES), F32), jax.ShapeDtypeStruct((1, LANES), F32)],
        scratch_shapes=[pltpu.VMEM((1, LANES), F32)],
        compiler_params=_cparams(("arbitrary",)),
        name="merge_router",
    )(o_sb, o_fx, x, g_sb, g_fx, w_out, g_ffn, w_r, b_r)


def _dispatch_plan(route, counts, tile, n_tiles_max):
    t = route.shape[0]
    expert = route[:, ROUTE_EXPERT:ROUTE_EXPERT + TOP_K].astype(I32)
    rank = route[:, ROUTE_RANK:ROUTE_RANK + TOP_K].astype(I32)
    count = counts[0, :N_EXPERTS].astype(I32)
    tiles = (count + tile - 1) // tile
    tile_end = jnp.cumsum(tiles)
    tile_start = tile_end - tiles
    pos = tile_start[expert] * tile + rank
    tile_expert = jnp.minimum(jnp.searchsorted(tile_end, jnp.arange(n_tiles_max, dtype=I32), side="right"),
                              N_EXPERTS - 1).astype(I32)
    token = jnp.broadcast_to(jnp.arange(t, dtype=I32)[:, None], (t, TOP_K))
    row_token = jnp.zeros((n_tiles_max * tile,), I32).at[pos.reshape(-1)].set(token.reshape(-1))
    return pos.reshape(-1), tile_expert, row_token, tile_end[-1:].astype(I32)


def _expert_kernel(te_ref, rt_ref, nt_ref, u_hbm, wg_ref, wu_ref, wd_ref, y_ref, xbuf, sem, wg_b, wu_b, wd_b,
                   *, tile):
    i = pl.program_id(0)
    nt = nt_ref[0]
    slot = lax.rem(i, 2)

    def gather(tile_idx, dst_slot):
        base = tile_idx * tile

        def body(r, carry):
            tok = rt_ref[base + r]
            pltpu.make_async_copy(u_hbm.at[pl.ds(tok, 1), :], xbuf.at[dst_slot, pl.ds(r, 1), :],
                                  sem.at[dst_slot]).start()
            return carry

        lax.fori_loop(0, tile, body, 0, unroll=8)

    @pl.when(i == 0)
    def _():
        gather(0, 0)

    @pl.when(i + 1 < nt)
    def _():
        gather(i + 1, 1 - slot)

    @pl.when(i < nt)
    def _():
        pltpu.make_async_copy(xbuf.at[slot], xbuf.at[slot], sem.at[slot]).wait()

        @pl.when((i == 0) | (te_ref[i] != te_ref[jnp.maximum(i - 1, 0)]))
        def _():
            wg_b[...] = wg_ref[...].astype(BF16)
            wu_b[...] = wu_ref[...].astype(BF16)
            wd_b[...] = wd_ref[...].astype(BF16)

        x = xbuf[slot].astype(BF16)
        gate = jnp.dot(x, wg_b[...], preferred_element_type=F32)
        up = jnp.dot(x, wu_b[...], preferred_element_type=F32)
        h = gate * (1.0 / (1.0 + jnp.exp(-gate))) * up
        y_ref[...] = jnp.dot(h.astype(BF16), wd_b[...], preferred_element_type=F32)

    @pl.when(i >= nt)
    def _():
        y_ref[...] = jnp.zeros_like(y_ref)


def _experts(u, tile_expert, row_token, n_tiles, w_gate, w_up, w_down, tile):
    d = u.shape[1]
    n_e, _, d_e = w_gate.shape
    n_tiles_max = tile_expert.shape[0]
    last = lambda i, nt: jnp.minimum(i, nt[0] - 1)
    grid_spec = pltpu.PrefetchScalarGridSpec(
        num_scalar_prefetch=3,
        grid=(n_tiles_max,),
        in_specs=[
            pl.BlockSpec(memory_space=pl.ANY),
            pl.BlockSpec((None, d, d_e), lambda i, te, rt, nt: (te[last(i, nt)], 0, 0)),
            pl.BlockSpec((None, d, d_e), lambda i, te, rt, nt: (te[last(i, nt)], 0, 0)),
            pl.BlockSpec((None, d_e, d), lambda i, te, rt, nt: (te[last(i, nt)], 0, 0)),
        ],
        out_specs=pl.BlockSpec((tile, d), lambda i, te, rt, nt: (i, 0)),
        scratch_shapes=[pltpu.VMEM((2, tile, d), F32), pltpu.SemaphoreType.DMA((2,)),
                        pltpu.VMEM((d, d_e), BF16), pltpu.VMEM((d, d_e), BF16), pltpu.VMEM((d_e, d), BF16)],
    )
    return pl.pallas_call(
        functools.partial(_expert_kernel, tile=tile),
        grid_spec=grid_spec,
        out_shape=jax.ShapeDtypeStruct((n_tiles_max * tile, d), F32),
        compiler_params=_cparams(("arbitrary",)),
        name="moe_experts",
    )(tile_expert, row_token, n_tiles, u, w_gate, w_up, w_down)


def _combine_kernel(pos_ref, rows_hbm, hp_ref, route_ref, gfin_ref, y_ref, buf, sem, *, tm):
    i = pl.program_id(0)
    n = pl.num_programs(0)
    slot = lax.rem(i, 2)

    def gather(tile_idx, dst_slot):
        base = tile_idx * tm * TOP_K

        def body(r, carry):
            for k in range(TOP_K):
                p = pos_ref[base + r * TOP_K + k]
                pltpu.make_async_copy(rows_hbm.at[pl.ds(p, 1), :], buf.at[dst_slot, k, pl.ds(r, 1), :],
                                      sem.at[dst_slot]).start()
            return carry

        lax.fori_loop(0, tm, body, 0, unroll=4)

    @pl.when(i == 0)
    def _():
        gather(0, 0)

    @pl.when(i + 1 < n)
    def _():
        gather(i + 1, 1 - slot)

    pltpu.make_async_copy(buf.at[slot], buf.at[slot], sem.at[slot]).wait()
    route = route_ref[...]
    g_a = route[:, ROUTE_GATE:ROUTE_GATE + 1]
    g_b = route[:, ROUTE_GATE + 1:ROUTE_GATE + 2]
    f = g_a * buf[slot, 0] + g_b * buf[slot, 1]
    y_ref[...] = _rms(hp_ref[...] + f, gfin_ref[...])


def _combine(pos, rows, hp, route, g_final, tm):
    t, d = hp.shape
    grid_spec = pltpu.PrefetchScalarGridSpec(
        num_scalar_prefetch=1,
        grid=(t // tm,),
        in_specs=[
            pl.BlockSpec(memory_space=pl.ANY),
            pl.BlockSpec((tm, d), lambda i, pos: (i, 0)),
            pl.BlockSpec((tm, LANES), lambda i, pos: (i, 0)),
            pl.BlockSpec((1, d), lambda i, pos: (0, 0)),
        ],
        out_specs=pl.BlockSpec((tm, d), lambda i, pos: (i, 0)),
        scratch_shapes=[pltpu.VMEM((2, TOP_K, tm, d), F32), pltpu.SemaphoreType.DMA((2,))],
    )
    return pl.pallas_call(
        functools.partial(_combine_kernel, tm=tm),
        grid_spec=grid_spec,
        out_shape=jax.ShapeDtypeStruct((t, d), F32),
        compiler_params=_cparams(("arbitrary",)),
        name="moe_combine",
    )(pos, rows, hp, route, g_final)


def _routed_moe(u, hp, route, counts, w_gate, w_up, w_down, g_final, tile, tm_combine):
    t = u.shape[0]
    n_tiles_max = (TOP_K * t + N_EXPERTS * (tile - 1)) // tile
    pos, tile_expert, row_token, n_tiles = _dispatch_plan(route, counts, tile, n_tiles_max)
    rows = _experts(u, tile_expert, row_token, n_tiles, w_gate, w_up, w_down, tile)
    return _combine(pos, rows, hp, route, g_final, tm_combine)


def _pad_lanes(a, n=LANES):
    return jnp.pad(a, ((0, 0), (0, n - a.shape[1])))


def kernel(x_prompt, x_sample, cache_sb_k, cache_sb_v, cache_fox_k, cache_fox_v, cache_fox_logf, page_table, norm_attn_g, w_in, b_forget, g_sb_out, g_fox_out, w_out, norm_ffn_g, w_router_group, b_router_group, w_router_expert, b_router_expert, w_expert_gate, w_expert_up, w_expert_down, norm_final_g):
    batch, seq, d = x_prompt.shape
    nb = x_sample.shape[0]
    assert w_in.shape[0] == 1, "one trunk layer"
    d_qkv = 3 * (N_SB + N_FOX) * HEAD_DIM

    w_main = w_in[0, :, :d_qkv].astype(BF16)
    w_f = _pad_lanes(w_in[0, :, d_qkv:]).astype(BF16)
    b_f = _pad_lanes(b_forget)
    g_attn = norm_attn_g
    w_out_b = w_out[0].astype(BF16)
    w_r = jnp.concatenate([w_router_expert[0].transpose(1, 0, 2).reshape(d, N_EXPERTS), w_router_group[0]], axis=1)
    w_r = _pad_lanes(w_r)
    w_r_hi = w_r.astype(BF16)
    w_r_lo = (w_r - w_r_hi.astype(F32)).astype(BF16)
    w_r2 = jnp.concatenate([w_r_hi, w_r_lo], axis=1)
    b_r = _pad_lanes(jnp.concatenate([b_router_expert[0].reshape(1, N_EXPERTS), b_router_group], axis=1))
    g_fin = norm_final_g.reshape(1, d)
    experts = (w_expert_gate[0], w_expert_up[0], w_expert_down[0])

    xp = x_prompt.reshape(batch * seq, d)
    z_p, logf_p = _inproj(xp, g_attn, w_main, w_f, b_f, tm=TM_INPROJ)
    logf_p = logf_p[:, :N_FOX].reshape(batch, seq, N_FOX)
    c_rows = _cumsum_rows(logf_p.transpose(0, 2, 1)).reshape(batch * N_FOX, 1, seq)
    o_sb = _sb_attention(z_p, batch, seq, TQ, TK, HEADS_PER_STEP)
    o_fx = _fox_attention(z_p, c_rows, batch, seq, TQ, TK, HEADS_PER_STEP)
    hp_p, u_p, route_p, count_p = _merge(o_sb, o_fx, xp, g_sb_out, g_fox_out, w_out_b, norm_ffn_g, w_r2, b_r,
                                         tm=TM_MERGE)
    y_prompt = _routed_moe(u_p, hp_p, route_p, count_p, *experts, g_fin, TILE_EXPERT, TM_COMBINE)
    y_prompt = y_prompt.reshape(batch, seq, d)

    xs = x_sample.reshape(nb, d)
    z_s, logf_s = _inproj(xs, g_attn, w_main, w_f, b_f, tm=nb)
    logf_s = logf_s[:, :N_FOX]
    per_head = lambda a: a.reshape(nb, N_SB, HEAD_DIM)
    page = cache_fox_logf.shape[2]
    logf_rows = cache_fox_logf[0].reshape(-1, 1, page * N_FOX)
    o_sb_s, o_fx_s = _decode_attention(
        page_table, per_head(z_s[0]), per_head(z_s[3]), per_head(z_s[4]), per_head(z_s[5]),
        logf_s.reshape(nb, N_FOX, 1), cache_sb_k[0], cache_sb_v[0], cache_fox_k[0], cache_fox_v[0],
        logf_rows, PAGES_PER_STEP)
    hp_s, u_s, route_s, count_s = _merge(o_sb_s.reshape(nb, -1), o_fx_s.reshape(nb, -1), xs, g_sb_out, g_fox_out,
                                         w_out_b, norm_ffn_g, w_r2, b_r, tm=nb)
    y_sample = _routed_moe(u_s, hp_s, route_s, count_s, *experts, g_fin, nb, nb).reshape(nb, 1, d)

    heads_p = lambda a: a.reshape(1, batch, seq, N_SB, HEAD_DIM)
    heads_s = lambda a: a.reshape(1, nb, 1, N_SB, HEAD_DIM)
    return (y_prompt, y_sample,
            heads_p(z_p[1]), heads_p(z_p[2]), heads_p(z_p[4]), heads_p(z_p[5]), logf_p[None],
            heads_s(z_s[1]), heads_s(z_s[2]), heads_s(z_s[4]), heads_s(z_s[5]),
            logf_s.reshape(1, nb, 1, N_FOX))
```

```python
import functools

import jax
import jax.numpy as jnp
from jax import lax
from jax.experimental import pallas as pl
from jax.experimental.pallas import tpu as pltpu

HEAD_DIM = 128
N_SB = 8
N_FOX = 8
N_GROUPS = 4
EXPERTS_PER_GROUP = 8
N_EXPERTS = N_GROUPS * EXPERTS_PER_GROUP
TOP_K = 2
SCALE = HEAD_DIM ** -0.5
LOG2E = 1.4426950408889634
RMS_EPS = 1e-6
LANES = 128
D_MODEL = 2048
SLAB_ROWS = D_MODEL // LANES
VMEM_LIMIT = 56 * 1024 * 1024
NEG_INF = float("-inf")
BF16 = jnp.bfloat16
F32 = jnp.float32
I32 = jnp.int32

TM_INPROJ = 1024
TQ = 256
TK = 256
HEADS_PER_STEP = 8
PAGES_PER_STEP = 8
TM_MERGE = 256
TILE_EXPERT = 256
TM_COMBINE = 256


def _cparams(sem):
    return pltpu.CompilerParams(dimension_semantics=sem, vmem_limit_bytes=VMEM_LIMIT)


def _rms(x, g):
    return x * lax.rsqrt(jnp.mean(x * x, axis=-1, keepdims=True) + RMS_EPS) * g


def _log_sigmoid(x):
    return jnp.minimum(x, 0.0) - jnp.log(1.0 + jnp.exp(-jnp.abs(x)))


def _split_terms(x, terms=2):
    pieces = []
    rest = x
    for _ in range(terms - 1):
        head = rest.astype(BF16).astype(F32)
        pieces.append(head)
        rest = rest - head
    pieces.append(rest)
    return pieces


def _log2_sigmoid_neg(z2):
    neg_abs = pltpu.bitcast(pltpu.bitcast(z2, jnp.uint32) | jnp.uint32(0x80000000), F32)
    return jnp.log(1.0 + jnp.exp2(neg_abs)) * (-LOG2E) - jnp.maximum(z2, 0.0)


def _split_hi_lo_lanes(x):
    hi = pltpu.bitcast(pltpu.bitcast(x, jnp.uint32) & jnp.uint32(0xFFFF0000), F32)
    return jnp.concatenate([hi, x - hi], axis=1).astype(BF16)


def _dot_nt(a, b):
    return lax.dot_general(a, b, (((1,), (1,)), ((), ())), preferred_element_type=F32)


def _inproj_kernel(x_ref, g_ref, w_ref, wf_ref, bf_ref, z_ref, logf_ref, kv_ref, u_ref):
    j = pl.program_id(1)
    tm = x_ref.shape[0]

    @pl.when(j == 0)
    def _():
        u = _rms(x_ref[...], g_ref[...]).astype(BF16)
        u_ref[...] = u
        f = jnp.dot(u, wf_ref[...], preferred_element_type=F32) + bf_ref[...]
        logf_ref[...] = _log_sigmoid(f)

    z_ref[...] = jnp.dot(u_ref[...], w_ref[...], preferred_element_type=F32)

    @pl.when((j == 1) | (j == 2) | (j >= 4))
    def _():
        for h in range(N_SB):
            kv_ref[pl.ds(h, tm, stride=N_SB), :] = z_ref[:, h * HEAD_DIM:(h + 1) * HEAD_DIM]


def _inproj(x, g, w_main, w_f, b_f, tm):
    t, d = x.shape
    n_out = w_main.shape[1] // 1024
    kv_slot = lambda j: j // 2 + j // 5
    return pl.pallas_call(
        _inproj_kernel,
        grid=(t // tm, n_out),
        in_specs=[
            pl.BlockSpec((tm, d), lambda i, j: (i, 0)),
            pl.BlockSpec((1, d), lambda i, j: (0, 0)),
            pl.BlockSpec((d, 1024), lambda i, j: (0, j)),
            pl.BlockSpec((d, LANES), lambda i, j: (0, 0)),
            pl.BlockSpec((1, LANES), lambda i, j: (0, 0)),
        ],
        out_specs=[
            pl.BlockSpec((None, tm, 1024), lambda i, j: (j, i, 0)),
            pl.BlockSpec((tm, LANES), lambda i, j: (i, 0)),
            pl.BlockSpec((None, tm * N_SB, HEAD_DIM), lambda i, j: (kv_slot(j), i, 0)),
        ],
        out_shape=[
            jax.ShapeDtypeStruct((n_out, t, 1024), F32),
            jax.ShapeDtypeStruct((t, LANES), F32),
            jax.ShapeDtypeStruct((4, t * N_SB, HEAD_DIM), F32),
        ],
        scratch_shapes=[pltpu.VMEM((tm, d), BF16)],
        compiler_params=_cparams(("arbitrary", "arbitrary")),
        name="inproj",
    )(x, g, w_main, w_f, b_f)


def _cumsum_kernel(x_ref, c_ref, *, chunk):
    s = x_ref.shape[-1]
    r = lax.broadcasted_iota(I32, (chunk, chunk), 0)
    c = lax.broadcasted_iota(I32, (chunk, chunk), 1)
    upper = (r <= c).astype(BF16)
    carry = jnp.zeros((x_ref.shape[0], 1), F32)
    for k in range(s // chunk):
        x = x_ref[:, k * chunk:(k + 1) * chunk]
        stacked = jnp.concatenate(_split_terms(x, 3), axis=0).astype(BF16)
        parts = jnp.dot(stacked, upper, preferred_element_type=F32)
        n = x.shape[0]
        cs = parts[:n] + parts[n:2 * n] + parts[2 * n:] + carry
        c_ref[:, k * chunk:(k + 1) * chunk] = cs
        carry = cs[:, chunk - 1:chunk]


def _cumsum_rows(x):
    b, h, s = x.shape
    return pl.pallas_call(
        functools.partial(_cumsum_kernel, chunk=256),
        grid=(b,),
        in_specs=[pl.BlockSpec((None, h, s), lambda i: (i, 0, 0))],
        out_specs=pl.BlockSpec((None, h, s), lambda i: (i, 0, 0)),
        out_shape=jax.ShapeDtypeStruct((b, h, s), F32),
        compiler_params=_cparams(("arbitrary",)),
        name="logf_cumsum",
    )(x)


def _sb_kernel(q_ref, k_ref, v_ref, o_ref, *, tq, tk, heads):
    i = pl.program_id(2)
    r = lax.broadcasted_iota(I32, (2 * tk, tk), 0)
    c = lax.broadcasted_iota(I32, (2 * tk, tk), 1)
    not_before2 = ((r & (tk - 1)) >= c).astype(BF16)
    strictly_causal = lax.broadcasted_iota(I32, (tq, tk), 1) < lax.broadcasted_iota(I32, (tq, tk), 0)
    cols = [slice(g * HEAD_DIM, (g + 1) * HEAD_DIM) for g in range(heads)]
    q = [(q_ref[:, cs] * (SCALE * LOG2E)).astype(BF16) for cs in cols]
    hs = range(heads)

    def block(j, carry, diag):
        start = pl.multiple_of(j * tk, tk)
        z = [_dot_nt(q[g], k_ref[pl.ds(start, tk), cols[g]].astype(BF16)) for g in hs]
        log_stay = [_log2_sigmoid_neg(z[g]) for g in hs]
        if diag:
            log_stay = [jnp.where(strictly_causal, x, 0.0) for x in log_stay]
        incl = [jnp.dot(_split_hi_lo_lanes(log_stay[g]), not_before2, preferred_element_type=F32) for g in hs]
        a = [jnp.exp2(z[g] + incl[g] + carry[g][0]) for g in hs]
        if diag:
            a = [jnp.where(strictly_causal, x, 0.0) for x in a]
        pv = [jnp.dot(a[g].astype(BF16), v_ref[pl.ds(start, tk), cols[g]].astype(BF16),
                      preferred_element_type=F32) for g in hs]
        return tuple((carry[g][0] + incl[g][:, 0:1], carry[g][1] + pv[g]) for g in hs)

    init = tuple((jnp.zeros((tq, 1), F32), jnp.zeros((tq, HEAD_DIM), F32)) for _ in hs)
    carry = block(i, init, True)
    carry = lax.fori_loop(0, i, lambda t, cy: block(i - 1 - t, cy, False), carry)
    for g in range(heads):
        o_ref[:, cols[g]] = carry[g][1]


def _sb_attention(z, batch, seq, tq, tk, heads):
    nq = seq // tq
    width = heads * HEAD_DIM
    return pl.pallas_call(
        functools.partial(_sb_kernel, tq=tq, tk=tk, heads=heads),
        grid=(batch, N_SB // heads, nq),
        in_specs=[
            pl.BlockSpec((None, tq, width), lambda b, h, i: (0, b * nq + i, h)),
            pl.BlockSpec((None, seq, width), lambda b, h, i: (1, b, h)),
            pl.BlockSpec((None, seq, width), lambda b, h, i: (2, b, h)),
        ],
        out_specs=pl.BlockSpec((tq, width), lambda b, h, i: (b * nq + i, h)),
        out_shape=jax.ShapeDtypeStruct((batch * seq, N_SB * HEAD_DIM), F32),
        compiler_params=_cparams(("arbitrary", "arbitrary", "arbitrary")),
        name="sb_attention",
    )(z, z, z)


def _fox_kernel(q_ref, k_ref, v_ref, c_ref, o_ref, *, tq, tk, heads):
    i = pl.program_id(2)
    causal = lax.broadcasted_iota(I32, (tq, tk), 1) <= lax.broadcasted_iota(I32, (tq, tk), 0)
    cols = [slice(g * HEAD_DIM, (g + 1) * HEAD_DIM) for g in range(heads)]
    q = [(q_ref[:, cs] * (SCALE * LOG2E)).astype(BF16) for cs in cols]
    q_start = pl.multiple_of(i * tq, tq)
    c_q0 = [c_ref[g, :, pl.ds(q_start, tq)][:, 0:1] for g in range(heads)]
    hs = range(heads)

    def block(j, carry, diag):
        start = pl.multiple_of(j * tk, tk)
        s = [_dot_nt(q[g], k_ref[pl.ds(start, tk), cols[g]].astype(BF16))
             + (c_q0[g] - c_ref[g, :, pl.ds(start, tk)]) * LOG2E for g in hs]
        if diag:
            s = [jnp.where(causal, x, NEG_INF) for x in s]
        m_new = [jnp.maximum(carry[g][0], jnp.max(s[g], axis=-1, keepdims=True)) for g in hs]
        p = [jnp.exp2(s[g] - m_new[g]) for g in hs]
        alpha = [jnp.exp2(carry[g][0] - m_new[g]) for g in hs]
        pv = [jnp.dot(p[g].astype(BF16), v_ref[pl.ds(start, tk), cols[g]].astype(BF16),
                      preferred_element_type=F32) for g in hs]
        return tuple((m_new[g], alpha[g] * carry[g][1] + jnp.sum(p[g], axis=-1, keepdims=True),
                      alpha[g] * carry[g][2] + pv[g]) for g in hs)

    init = tuple((jnp.full((tq, 1), NEG_INF, F32), jnp.zeros((tq, 1), F32), jnp.zeros((tq, HEAD_DIM), F32))
                 for _ in hs)
    carry = block(i, init, True)
    carry = lax.fori_loop(0, i, lambda t, cy: block(i - 1 - t, cy, False), carry)
    for g in range(heads):
        o_ref[:, cols[g]] = carry[g][2] / carry[g][1]


def _fox_attention(z, c_rows, batch, seq, tq, tk, heads):
    nq = seq // tq
    width = heads * HEAD_DIM
    groups = N_FOX // heads
    return pl.pallas_call(
        functools.partial(_fox_kernel, tq=tq, tk=tk, heads=heads),
        grid=(batch, groups, nq),
        in_specs=[
            pl.BlockSpec((None, tq, width), lambda b, h, i: (3, b * nq + i, h)),
            pl.BlockSpec((None, seq, width), lambda b, h, i: (4, b, h)),
            pl.BlockSpec((None, seq, width), lambda b, h, i: (5, b, h)),
            pl.BlockSpec((heads, 1, seq), lambda b, h, i: (b * groups + h, 0, 0)),
        ],
        out_specs=pl.BlockSpec((tq, width), lambda b, h, i: (b * nq + i, h)),
        out_shape=jax.ShapeDtypeStruct((batch * seq, N_FOX * HEAD_DIM), F32),
        compiler_params=_cparams(("arbitrary", "arbitrary", "arbitrary")),
        name="fox_attention",
    )(z, z, z, c_rows)


def _reverse_cumsum_lanes(x, lane):
    n = x.shape[-1]
    d = N_SB
    while d < n:
        shifted = pltpu.roll(x, n - d, 1)
        x = x + jnp.where(lane + d < n, shifted, 0.0)
        d *= 2
    return x


def _decode_kernel(pt_ref, qs_ref, qf_ref, kown_ref, vown_ref, lfown_ref, *refs, pages_per_step):
    del pt_ref
    pp = pages_per_step
    ksb = refs[0:pp]
    vsb = refs[pp:2 * pp]
    kfx = refs[2 * pp:3 * pp]
    vfx = refs[3 * pp:4 * pp]
    lfc = refs[4 * pp:5 * pp]
    osb_ref, ofx_ref = refs[5 * pp:5 * pp + 2]
    acc_sb, acc_fx, run_sb, run_fx, m_ref, l_ref = refs[5 * pp + 2:]
    j = pl.program_id(1)
    page_rows = ksb[0].shape[0] * ksb[0].shape[1]
    rows = pp * page_rows

    qs = qs_ref[...] * SCALE
    qf = qf_ref[...] * SCALE

    @pl.when(j == 0)
    def _():
        acc_sb[...] = jnp.zeros_like(acc_sb)
        run_sb[...] = jnp.zeros_like(run_sb)
        m_ref[...] = jnp.sum(qf * kown_ref[...], axis=-1, keepdims=True)
        l_ref[...] = jnp.ones_like(l_ref)
        acc_fx[...] = vown_ref[...]
        run_fx[...] = lfown_ref[...]

    def rows_of(page_refs):
        return jnp.concatenate([page_refs[p][...].reshape(page_rows, HEAD_DIM).astype(BF16)
                                for p in reversed(range(pp))], axis=0)

    lane = lax.broadcasted_iota(I32, (N_SB, rows), 1)
    head = lax.broadcasted_iota(I32, (N_SB, rows), 0)
    valid = (lane & (N_SB - 1)) == head

    z = _dot_nt(qs.astype(BF16), rows_of(ksb))
    log_stay = jnp.where(valid, _log_sigmoid(-z), 0.0)
    incl = _reverse_cumsum_lanes(log_stay, lane)
    run = run_sb[...]
    a = jnp.where(valid, jnp.exp(z + incl + run), 0.0)
    acc_sb[...] += jnp.dot(a.astype(BF16), rows_of(vsb), preferred_element_type=F32)
    run_sb[...] = run + jnp.sum(log_stay, axis=-1, keepdims=True)

    lf = jnp.where(valid, jnp.concatenate([lfc[p][...] for p in reversed(range(pp))], axis=1), 0.0)
    incl = _reverse_cumsum_lanes(lf, lane)
    run = run_fx[...]
    s = jnp.where(valid, _dot_nt(qf.astype(BF16), rows_of(kfx)) + (incl - lf) + run, NEG_INF)
    m_old = m_ref[...]
    m_new = jnp.maximum(m_old, jnp.max(s, axis=-1, keepdims=True))
    alpha = jnp.exp(m_old - m_new)
    pr = jnp.exp(s - m_new)
    l_ref[...] = alpha * l_ref[...] + jnp.sum(pr, axis=-1, keepdims=True)
    acc_fx[...] = alpha * acc_fx[...] + jnp.dot(pr.astype(BF16), rows_of(vfx), preferred_element_type=F32)
    m_ref[...] = m_new
    run_fx[...] = run + jnp.sum(lf, axis=-1, keepdims=True)

    @pl.when(j == pl.num_programs(1) - 1)
    def _():
        osb_ref[...] = acc_sb[...]
        ofx_ref[...] = acc_fx[...] / l_ref[...]


def _decode_attention(page_table, q_sb, q_fx, k_own, v_own, lf_own, c_sb_k, c_sb_v, c_fx_k, c_fx_v,
                      c_logf_rows, pages_per_step):
    nb, n_pages = page_table.shape
    page, heads, hd = c_sb_k.shape[1:]
    rows = page * heads
    pp = pages_per_step
    steps = n_pages // pp

    def page_of(p):
        return lambda b, j, pt: (pt[b * n_pages + (n_pages - 1 - (j * pp + p))], 0, 0, 0)

    def lf_page_of(p):
        return lambda b, j, pt: (pt[b * n_pages + (n_pages - 1 - (j * pp + p))], 0, 0)

    per_b = pl.BlockSpec((None, heads, hd), lambda b, j, pt: (b, 0, 0))
    cache_specs = [pl.BlockSpec((None, page, heads, hd), page_of(p)) for p in range(pp)]
    lf_specs = [pl.BlockSpec((None, 1, rows), lf_page_of(p)) for p in range(pp)]
    grid_spec = pltpu.PrefetchScalarGridSpec(
        num_scalar_prefetch=1,
        grid=(nb, steps),
        in_specs=[per_b, per_b, per_b, per_b, pl.BlockSpec((None, heads, 1), lambda b, j, pt: (b, 0, 0))]
        + cache_specs * 4 + lf_specs,
        out_specs=[per_b, per_b],
        scratch_shapes=[pltpu.VMEM((heads, hd), F32), pltpu.VMEM((heads, hd), F32),
                        pltpu.VMEM((heads, 1), F32), pltpu.VMEM((heads, 1), F32),
                        pltpu.VMEM((heads, 1), F32), pltpu.VMEM((heads, 1), F32)],
    )
    return pl.pallas_call(
        functools.partial(_decode_kernel, pages_per_step=pp),
        grid_spec=grid_spec,
        out_shape=[jax.ShapeDtypeStruct((nb, heads, hd), F32)] * 2,
        compiler_params=_cparams(("arbitrary", "arbitrary")),
        name="decode_attention",
    )(page_table.reshape(-1), q_sb, q_fx, k_own, v_own, lf_own,
      *([c_sb_k] * pp), *([c_sb_v] * pp), *([c_fx_k] * pp), *([c_fx_v] * pp), *([c_logf_rows] * pp))


ROUTE_EXPERT, ROUTE_GATE, ROUTE_RANK = 0, 2, 4


def _merge_kernel(osb_ref, ofx_ref, x_ref, gsb_ref, gfx_ref, wout_ref, gffn_ref, wr_ref, br_ref,
                  hp_ref, u_ref, route_ref, count_ref, base_ref):
    @pl.when(pl.program_id(0) == 0)
    def _():
        base_ref[...] = jnp.zeros_like(base_ref)

    o = jnp.concatenate([_rms(osb_ref[...], gsb_ref[...]), _rms(ofx_ref[...], gfx_ref[...])], axis=-1)
    y = jnp.dot(o.astype(BF16), wout_ref[...], preferred_element_type=F32)
    hp = x_ref[...] + y
    hp_ref[...] = hp
    u = _rms(hp, gffn_ref[...])
    tm = u.shape[0]
    for s in range(SLAB_ROWS):
        u_ref[pl.ds(s, tm, stride=SLAB_ROWS), :] = u[:, s * LANES:(s + 1) * LANES]

    stacked = jnp.concatenate(_split_terms(u), axis=0).astype(BF16)
    parts = jnp.dot(stacked, wr_ref[...], preferred_element_type=F32)
    both = parts[:tm] + parts[tm:]
    logits = both[:, :LANES] + both[:, LANES:] + br_ref[...]

    lane = lax.broadcasted_iota(I32, (tm, LANES), 1).astype(F32)
    is_group = (lane >= N_EXPERTS) & (lane < N_EXPERTS + N_GROUPS)
    big = float(1 << 20)
    lg = jnp.where(is_group, logits, NEG_INF)
    lg_max = jnp.max(lg, axis=-1, keepdims=True)
    gsel = jnp.min(jnp.where(lg == lg_max, lane, big), axis=-1, keepdims=True) - N_EXPERTS
    g1 = 1.0 / jnp.sum(jnp.exp(lg - lg_max), axis=-1, keepdims=True)

    in_group = (lane >= gsel * EXPERTS_PER_GROUP) & (lane < (gsel + 1) * EXPERTS_PER_GROUP)
    le = jnp.where(in_group, logits, NEG_INF)
    v_a = jnp.max(le, axis=-1, keepdims=True)
    i_a = jnp.min(jnp.where(le == v_a, lane, big), axis=-1, keepdims=True)
    le_b = jnp.where(lane == i_a, NEG_INF, le)
    v_b = jnp.max(le_b, axis=-1, keepdims=True)
    i_b = jnp.min(jnp.where(le_b == v_b, lane, big), axis=-1, keepdims=True)
    e_b = jnp.exp(v_b - v_a)
    gate_a = g1 / (1.0 + e_b)
    gate_b = g1 * e_b / (1.0 + e_b)

    onehot = jnp.where((lane == i_a) | (lane == i_b), 1.0, 0.0)
    rr = lax.broadcasted_iota(I32, (tm, tm), 0)
    cc = lax.broadcasted_iota(I32, (tm, tm), 1)
    count_dtype = BF16 if tm % 16 == 0 else F32
    earlier = (cc < rr).astype(count_dtype)
    rank_all = jnp.dot(earlier, onehot.astype(count_dtype), preferred_element_type=F32) + base_ref[...]
    rank_a = jnp.sum(jnp.where(lane == i_a, rank_all, 0.0), axis=-1, keepdims=True)
    rank_b = jnp.sum(jnp.where(lane == i_b, rank_all, 0.0), axis=-1, keepdims=True)
    total = base_ref[...] + jnp.sum(onehot, axis=0, keepdims=True)
    base_ref[...] = total
    count_ref[...] = total

    route = jnp.zeros((tm, LANES), F32)
    for k, val in ((ROUTE_EXPERT, i_a), (ROUTE_EXPERT + 1, i_b), (ROUTE_GATE, gate_a), (ROUTE_GATE + 1, gate_b),
                   (ROUTE_RANK, rank_a), (ROUTE_RANK + 1, rank_b)):
        route = jnp.where(lane == k, val, route)
    route_ref[...] = route


def _merge(o_sb, o_fx, x, g_sb, g_fx, w_out, g_ffn, w_r, b_r, tm):
    t, d = x.shape
    row = lambda n: pl.BlockSpec((tm, n), lambda i: (i, 0))
    const = lambda a: pl.BlockSpec(a.shape, lambda i: (0,) * a.ndim)
    return pl.pallas_call(
        _merge_kernel,
        grid=(t // tm,),
        in_specs=[row(o_sb.shape[1]), row(o_fx.shape[1]), row(d), const(g_sb), const(g_fx), const(w_out),
                  const(g_ffn), const(w_r), const(b_r)],
        out_specs=[row(d), pl.BlockSpec((tm * SLAB_ROWS, LANES), lambda i: (i, 0)), row(LANES),
                   pl.BlockSpec((1, LANES), lambda i: (0, 0))],
        out_shape=[jax.ShapeDtypeStruct((t, d), F32), jax.ShapeDtypeStruct((t * SLAB_ROWS, LANES), F32),
                   jax.ShapeDtypeStruct((t, LANES), F32), jax.ShapeDtypeStruct((1, LANES), F32)],
        scratch_shapes=[pltpu.VMEM((1, LANES), F32)],
        compiler_params=_cparams(("arbitrary",)),
        name="merge_router",
    )(o_sb, o_fx, x, g_sb, g_fx, w_out, g_ffn, w_r, b_r)


def _dispatch_plan(route, counts, tile, n_tiles_max):
    t = route.shape[0]
    expert = route[:, ROUTE_EXPERT:ROUTE_EXPERT + TOP_K].astype(I32)
    rank = route[:, ROUTE_RANK:ROUTE_RANK + TOP_K].astype(I32)
    count = counts[0, :N_EXPERTS].astype(I32)
    tiles = (count + tile - 1) // tile
    tile_end = jnp.cumsum(tiles)
    tile_start = tile_end - tiles
    pos = tile_start[expert] * tile + rank
    tile_ids = jnp.arange(n_tiles_max, dtype=I32)
    tile_expert = jnp.minimum(jnp.sum((tile_end[None, :] <= tile_ids[:, None]).astype(I32), axis=1), N_EXPERTS - 1)
    token = jnp.broadcast_to(jnp.arange(t, dtype=I32)[:, None], (t, TOP_K))
    row_token = jnp.zeros((n_tiles_max * tile,), I32).at[pos.reshape(-1)].set(token.reshape(-1), unique_indices=True)
    return pos.reshape(-1), tile_expert, row_token, tile_end[-1:].astype(I32)


def _slab_copy(src_hbm, src_row, dst, dst_row, sem):
    return pltpu.make_async_copy(src_hbm.at[pl.ds(pl.multiple_of(src_row * SLAB_ROWS, SLAB_ROWS), SLAB_ROWS), :],
                                 dst.at[pl.ds(pl.multiple_of(dst_row * SLAB_ROWS, SLAB_ROWS), SLAB_ROWS), :], sem)


def _rows_from_slabs(slabs, n):
    return jnp.concatenate([slabs[pl.ds(s, n, stride=SLAB_ROWS), :] for s in range(SLAB_ROWS)], axis=1)


def _expert_kernel(te_ref, rt_ref, nt_ref, u_hbm, wg_ref, wu_ref, wd_ref, y_ref, xbuf, sem, wg_b, wu_b, wd_b,
                   *, tile):
    i = pl.program_id(0)
    nt = nt_ref[0]
    slot = lax.rem(i, 2)

    def gather(tile_idx, dst_slot):
        base = tile_idx * tile

        def body(r, carry):
            _slab_copy(u_hbm, rt_ref[base + r], xbuf.at[dst_slot], r, sem.at[dst_slot]).start()
            return carry

        lax.fori_loop(0, tile, body, 0, unroll=8)

    @pl.when(i == 0)
    def _():
        gather(0, 0)

    @pl.when(i + 1 < nt)
    def _():
        gather(i + 1, 1 - slot)

    @pl.when(i < nt)
    def _():
        pltpu.make_async_copy(xbuf.at[slot], xbuf.at[slot], sem.at[slot]).wait()

        @pl.when((i == 0) | (te_ref[i] != te_ref[jnp.maximum(i - 1, 0)]))
        def _():
            wg_b[...] = wg_ref[...].astype(BF16)
            wu_b[...] = wu_ref[...].astype(BF16)
            wd_b[...] = wd_ref[...].astype(BF16)

        x = _rows_from_slabs(xbuf.at[slot], tile).astype(BF16)
        gate = jnp.dot(x, wg_b[...], preferred_element_type=F32)
        up = jnp.dot(x, wu_b[...], preferred_element_type=F32)
        h = gate * (1.0 / (1.0 + jnp.exp(-gate))) * up
        y = jnp.dot(h.astype(BF16), wd_b[...], preferred_element_type=F32)
        for s in range(SLAB_ROWS):
            y_ref[pl.ds(s, tile, stride=SLAB_ROWS), :] = y[:, s * LANES:(s + 1) * LANES]

    @pl.when(i >= nt)
    def _():
        y_ref[...] = jnp.zeros_like(y_ref)


def _experts(u_slabs, tile_expert, row_token, n_tiles, w_gate, w_up, w_down, tile):
    n_e, d, d_e = w_gate.shape
    n_tiles_max = tile_expert.shape[0]
    last = lambda i, nt: jnp.minimum(i, nt[0] - 1)
    grid_spec = pltpu.PrefetchScalarGridSpec(
        num_scalar_prefetch=3,
        grid=(n_tiles_max,),
        in_specs=[
            pl.BlockSpec(memory_space=pl.ANY),
            pl.BlockSpec((None, d, d_e), lambda i, te, rt, nt: (te[last(i, nt)], 0, 0)),
            pl.BlockSpec((None, d, d_e), lambda i, te, rt, nt: (te[last(i, nt)], 0, 0)),
            pl.BlockSpec((None, d_e, d), lambda i, te, rt, nt: (te[last(i, nt)], 0, 0)),
        ],
        out_specs=pl.BlockSpec((tile * SLAB_ROWS, LANES), lambda i, te, rt, nt: (i, 0)),
        scratch_shapes=[pltpu.VMEM((2, tile * SLAB_ROWS, LANES), F32), pltpu.SemaphoreType.DMA((2,)),
                        pltpu.VMEM((d, d_e), BF16), pltpu.VMEM((d, d_e), BF16), pltpu.VMEM((d_e, d), BF16)],
    )
    return pl.pallas_call(
        functools.partial(_expert_kernel, tile=tile),
        grid_spec=grid_spec,
        out_shape=jax.ShapeDtypeStruct((n_tiles_max * tile * SLAB_ROWS, LANES), F32),
        compiler_params=_cparams(("arbitrary",)),
        name="moe_experts",
    )(tile_expert, row_token, n_tiles, u_slabs, w_gate, w_up, w_down)


def _combine_kernel(pos_ref, rows_hbm, hp_ref, route_ref, gfin_ref, y_ref, buf, sem, *, tm):
    i = pl.program_id(0)
    n = pl.num_programs(0)
    slot = lax.rem(i, 2)

    def gather(tile_idx, dst_slot):
        base = tile_idx * tm * TOP_K

        def body(r, carry):
            for k in range(TOP_K):
                _slab_copy(rows_hbm, pos_ref[base + r * TOP_K + k], buf.at[dst_slot, k], r, sem.at[dst_slot]).start()
            return carry

        lax.fori_loop(0, tm, body, 0, unroll=4)

    @pl.when(i == 0)
    def _():
        gather(0, 0)

    @pl.when(i + 1 < n)
    def _():
        gather(i + 1, 1 - slot)

    pltpu.make_async_copy(buf.at[slot], buf.at[slot], sem.at[slot]).wait()
    route = route_ref[...]
    g_a = route[:, ROUTE_GATE:ROUTE_GATE + 1]
    g_b = route[:, ROUTE_GATE + 1:ROUTE_GATE + 2]
    f = g_a * _rows_from_slabs(buf.at[slot, 0], tm) + g_b * _rows_from_slabs(buf.at[slot, 1], tm)
    y_ref[...] = _rms(hp_ref[...] + f, gfin_ref[...])


def _combine(pos, rows, hp, route, g_final, tm):
    t, d = hp.shape
    grid_spec = pltpu.PrefetchScalarGridSpec(
        num_scalar_prefetch=1,
        grid=(t // tm,),
        in_specs=[
            pl.BlockSpec(memory_space=pl.ANY),
            pl.BlockSpec((tm, d), lambda i, pos: (i, 0)),
            pl.BlockSpec((tm, LANES), lambda i, pos: (i, 0)),
            pl.BlockSpec((1, d), lambda i, pos: (0, 0)),
        ],
        out_specs=pl.BlockSpec((tm, d), lambda i, pos: (i, 0)),
        scratch_shapes=[pltpu.VMEM((2, TOP_K, tm * SLAB_ROWS, LANES), F32), pltpu.SemaphoreType.DMA((2,))],
    )
    return pl.pallas_call(
        functools.partial(_combine_kernel, tm=tm),
        grid_spec=grid_spec,
        out_shape=jax.ShapeDtypeStruct((t, d), F32),
        compiler_params=_cparams(("arbitrary",)),
        name="moe_combine",
    )(pos, rows, hp, route, g_final)


def _routed_moe(u, hp, route, counts, w_gate, w_up, w_down, g_final, tile, tm_combine):
    t = hp.shape[0]
    n_tiles_max = (TOP_K * t + N_EXPERTS * (tile - 1)) // tile
    pos, tile_expert, row_token, n_tiles = _dispatch_plan(route, counts, tile, n_tiles_max)
    rows = _experts(u, tile_expert, row_token, n_tiles, w_gate, w_up, w_down, tile)
    return _combine(pos, rows, hp, route, g_final, tm_combine)


def _pad_lanes(a, n=LANES):
    return jnp.pad(a, ((0, 0), (0, n - a.shape[1])))


def kernel(x_prompt, x_sample, cache_sb_k, cache_sb_v, cache_fox_k, cache_fox_v, cache_fox_logf, page_table, norm_attn_g, w_in, b_forget, g_sb_out, g_fox_out, w_out, norm_ffn_g, w_router_group, b_router_group, w_router_expert, b_router_expert, w_expert_gate, w_expert_up, w_expert_down, norm_final_g):
    batch, seq, d = x_prompt.shape
    nb = x_sample.shape[0]
    assert w_in.shape[0] == 1, "one trunk layer"
    d_qkv = 3 * (N_SB + N_FOX) * HEAD_DIM

    w_main = w_in[0, :, :d_qkv].astype(BF16)
    w_f = _pad_lanes(w_in[0, :, d_qkv:]).astype(BF16)
    b_f = _pad_lanes(b_forget)
    g_attn = norm_attn_g
    w_out_b = w_out[0].astype(BF16)
    w_r = jnp.concatenate([w_router_expert[0].transpose(1, 0, 2).reshape(d, N_EXPERTS), w_router_group[0]], axis=1)
    w_r = _pad_lanes(w_r)
    w_r_hi = w_r.astype(BF16)
    w_r_lo = (w_r - w_r_hi.astype(F32)).astype(BF16)
    w_r2 = jnp.concatenate([w_r_hi, w_r_lo], axis=1)
    b_r = _pad_lanes(jnp.concatenate([b_router_expert[0].reshape(1, N_EXPERTS), b_router_group], axis=1))
    g_fin = norm_final_g.reshape(1, d)
    experts = (w_expert_gate[0], w_expert_up[0], w_expert_down[0])

    xp = x_prompt.reshape(batch * seq, d)
    z_p, logf_p, kv_p = _inproj(xp, g_attn, w_main, w_f, b_f, tm=TM_INPROJ)
    logf_p = logf_p[:, :N_FOX].reshape(batch, seq, N_FOX)
    c_rows = _cumsum_rows(logf_p.transpose(0, 2, 1)).reshape(batch * N_FOX, 1, seq)
    o_sb = _sb_attention(z_p, batch, seq, TQ, TK, HEADS_PER_STEP)
    o_fx = _fox_attention(z_p, c_rows, batch, seq, TQ, TK, HEADS_PER_STEP)
    hp_p, u_p, route_p, count_p = _merge(o_sb, o_fx, xp, g_sb_out, g_fox_out, w_out_b, norm_ffn_g, w_r2, b_r,
                                         tm=TM_MERGE)
    y_prompt = _routed_moe(u_p, hp_p, route_p, count_p, *experts, g_fin, TILE_EXPERT, TM_COMBINE)
    y_prompt = y_prompt.reshape(batch, seq, d)

    xs = x_sample.reshape(nb, d)
    z_s, logf_s, kv_s = _inproj(xs, g_attn, w_main, w_f, b_f, tm=nb)
    logf_s = logf_s[:, :N_FOX]
    per_head = lambda a: a.reshape(nb, N_SB, HEAD_DIM)
    page = cache_fox_logf.shape[2]
    logf_rows = cache_fox_logf[0].reshape(-1, 1, page * N_FOX)
    o_sb_s, o_fx_s = _decode_attention(
        page_table, per_head(z_s[0]), per_head(z_s[3]), per_head(z_s[4]), per_head(z_s[5]),
        logf_s.reshape(nb, N_FOX, 1), cache_sb_k[0], cache_sb_v[0], cache_fox_k[0], cache_fox_v[0],
        logf_rows, PAGES_PER_STEP)
    hp_s, u_s, route_s, count_s = _merge(o_sb_s.reshape(nb, -1), o_fx_s.reshape(nb, -1), xs, g_sb_out, g_fox_out,
                                         w_out_b, norm_ffn_g, w_r2, b_r, tm=nb)
    y_sample = _routed_moe(u_s, hp_s, route_s, count_s, *experts, g_fin, nb, nb).reshape(nb, 1, d)

    heads_p = lambda a: a.reshape(1, batch, seq, N_SB, HEAD_DIM)
    heads_s = lambda a: a.reshape(1, nb, 1, N_SB, HEAD_DIM)
    return (y_prompt, y_sample,
            heads_p(kv_p[0]), heads_p(kv_p[1]), heads_p(kv_p[2]), heads_p(kv_p[3]), logf_p[None],
            heads_s(kv_s[0]), heads_s(kv_s[1]), heads_s(kv_s[2]), heads_s(kv_s[3]),
            logf_s.reshape(1, nb, 1, N_FOX))
```

```python
import functools

import jax
import jax.numpy as jnp
from jax import lax
from jax.experimental import pallas as pl
from jax.experimental.pallas import tpu as pltpu

HEAD_DIM = 128
N_SB = 8
N_FOX = 8
N_GROUPS = 4
EXPERTS_PER_GROUP = 8
N_EXPERTS = N_GROUPS * EXPERTS_PER_GROUP
TOP_K = 2
SCALE = HEAD_DIM ** -0.5
LOG2E = 1.4426950408889634
RMS_EPS = 1e-6
LANES = 128
D_MODEL = 2048
QKV_BLOCK = N_SB * HEAD_DIM
SLAB_ROWS = D_MODEL // LANES
HALF_ROWS = SLAB_ROWS // 2
VMEM_LIMIT = 56 * 1024 * 1024
NEG_INF = float("-inf")
BF16 = jnp.bfloat16
F32 = jnp.float32
I32 = jnp.int32

TM_INPROJ = 512
TQ = 256
TK = 256
HEADS_PER_STEP = 8
PAGES_PER_STEP = 8
TM_MERGE = 256
TILE_EXPERT = 256
TM_COMBINE = 256


def _cparams(sem):
    return pltpu.CompilerParams(dimension_semantics=sem, vmem_limit_bytes=VMEM_LIMIT)


def _rms(x, g):
    return x * lax.rsqrt(jnp.mean(x * x, axis=-1, keepdims=True) + RMS_EPS) * g


def _log_sigmoid(x):
    return jnp.minimum(x, 0.0) - jnp.log(1.0 + jnp.exp(-jnp.abs(x)))


def _split_terms(x, terms=2):
    pieces = []
    rest = x
    for _ in range(terms - 1):
        head = rest.astype(BF16).astype(F32)
        pieces.append(head)
        rest = rest - head
    pieces.append(rest)
    return pieces


def _log2_sigmoid_neg(z2):
    neg_abs = pltpu.bitcast(pltpu.bitcast(z2, jnp.uint32) | jnp.uint32(0x80000000), F32)
    return jnp.log(1.0 + jnp.exp2(neg_abs)) * (-LOG2E) - jnp.maximum(z2, 0.0)


def _split_hi_lo_lanes(x):
    hi = pltpu.bitcast(pltpu.bitcast(x, jnp.uint32) & jnp.uint32(0xFFFF0000), F32)
    return jnp.concatenate([hi, x - hi], axis=1).astype(BF16)


def _dot_nt(a, b):
    return lax.dot_general(a, b, (((1,), (1,)), ((), ())), preferred_element_type=F32)


def _inproj_kernel(x_ref, g_ref, w_ref, wf_ref, bf_ref, q_ref, logf_ref, ksb_ref, vsb_ref, kfx_ref, vfx_ref,
                   u_ref, z_ref):
    j = pl.program_id(1)
    tm = x_ref.shape[0]

    @pl.when(j == 0)
    def _():
        u = _rms(x_ref[...], g_ref[...]).astype(BF16)
        u_ref[...] = u
        f = jnp.dot(u, wf_ref[...], preferred_element_type=F32) + bf_ref[...]
        logf_ref[...] = _log_sigmoid(f)

    z_ref[...] = jnp.dot(u_ref[...], w_ref[...], preferred_element_type=F32)

    @pl.when((j == 0) | (j == 3))
    def _():
        q_ref[...] = z_ref[...]

    for block, ref in ((1, ksb_ref), (2, vsb_ref), (4, kfx_ref), (5, vfx_ref)):
        @pl.when(j == block)
        def _(ref=ref):
            for h in range(N_SB):
                ref[pl.ds(h, tm, stride=N_SB), :] = z_ref[:, h * HEAD_DIM:(h + 1) * HEAD_DIM]


def _inproj(x, g, w_main, w_f, b_f, tm):
    t, d = x.shape
    n_blocks = w_main.shape[1] // QKV_BLOCK
    kv_spec = pl.BlockSpec((tm * N_SB, HEAD_DIM), lambda i, j: (i, 0))
    kv_shape = jax.ShapeDtypeStruct((t * N_SB, HEAD_DIM), F32)
    return pl.pallas_call(
        _inproj_kernel,
        grid=(t // tm, n_blocks),
        in_specs=[
            pl.BlockSpec((tm, d), lambda i, j: (i, 0)),
            pl.BlockSpec((1, d), lambda i, j: (0, 0)),
            pl.BlockSpec((d, QKV_BLOCK), lambda i, j: (0, j)),
            pl.BlockSpec((d, LANES), lambda i, j: (0, 0)),
            pl.BlockSpec((1, LANES), lambda i, j: (0, 0)),
        ],
        out_specs=[
            pl.BlockSpec((None, tm, QKV_BLOCK), lambda i, j: (j // 3, i, 0)),
            pl.BlockSpec((tm, LANES), lambda i, j: (i, 0)),
            kv_spec, kv_spec, kv_spec, kv_spec,
        ],
        out_shape=[
            jax.ShapeDtypeStruct((2, t, QKV_BLOCK), F32),
            jax.ShapeDtypeStruct((t, LANES), F32),
            kv_shape, kv_shape, kv_shape, kv_shape,
        ],
        scratch_shapes=[pltpu.VMEM((tm, d), BF16), pltpu.VMEM((tm, QKV_BLOCK), F32)],
        compiler_params=_cparams(("arbitrary", "arbitrary")),
        name="inproj",
    )(x, g, w_main, w_f, b_f)


def _cumsum_kernel(x_ref, c_ref, *, chunk):
    s = x_ref.shape[-1]
    r = lax.broadcasted_iota(I32, (chunk, chunk), 0)
    c = lax.broadcasted_iota(I32, (chunk, chunk), 1)
    upper = (r <= c).astype(BF16)
    carry = jnp.zeros((x_ref.shape[0], 1), F32)
    for k in range(s // chunk):
        x = x_ref[:, k * chunk:(k + 1) * chunk]
        stacked = jnp.concatenate(_split_terms(x, 3), axis=0).astype(BF16)
        parts = jnp.dot(stacked, upper, preferred_element_type=F32)
        n = x.shape[0]
        cs = parts[:n] + parts[n:2 * n] + parts[2 * n:] + carry
        c_ref[:, k * chunk:(k + 1) * chunk] = cs
        carry = cs[:, chunk - 1:chunk]


def _cumsum_rows(x):
    b, h, s = x.shape
    return pl.pallas_call(
        functools.partial(_cumsum_kernel, chunk=256),
        grid=(b,),
        in_specs=[pl.BlockSpec((None, h, s), lambda i: (i, 0, 0))],
        out_specs=pl.BlockSpec((None, h, s), lambda i: (i, 0, 0)),
        out_shape=jax.ShapeDtypeStruct((b, h, s), F32),
        compiler_params=_cparams(("arbitrary",)),
        name="logf_cumsum",
    )(x)


def _head_rows(ref, start, n, head, heads):
    return ref[pl.ds(pl.multiple_of(start * heads, heads) + head, n, stride=heads), :].astype(BF16)


def _sb_kernel(q_ref, k_ref, v_ref, o_ref, *, tq, tk, heads):
    i = pl.program_id(1)
    r = lax.broadcasted_iota(I32, (2 * tk, tk), 0)
    c = lax.broadcasted_iota(I32, (2 * tk, tk), 1)
    not_before2 = ((r & (tk - 1)) >= c).astype(BF16)
    strictly_causal = lax.broadcasted_iota(I32, (tq, tk), 1) < lax.broadcasted_iota(I32, (tq, tk), 0)
    cols = [slice(g * HEAD_DIM, (g + 1) * HEAD_DIM) for g in range(heads)]
    q = [(q_ref[:, cs] * (SCALE * LOG2E)).astype(BF16) for cs in cols]
    hs = range(heads)

    def block(j, carry, diag):
        start = pl.multiple_of(j * tk, tk)
        z = [_dot_nt(q[g], _head_rows(k_ref, start, tk, g, heads)) for g in hs]
        log_stay = [_log2_sigmoid_neg(z[g]) for g in hs]
        if diag:
            log_stay = [jnp.where(strictly_causal, x, 0.0) for x in log_stay]
        incl = [jnp.dot(_split_hi_lo_lanes(log_stay[g]), not_before2, preferred_element_type=F32) for g in hs]
        a = [jnp.exp2(z[g] + incl[g] + carry[g][0]) for g in hs]
        if diag:
            a = [jnp.where(strictly_causal, x, 0.0) for x in a]
        pv = [jnp.dot(a[g].astype(BF16), _head_rows(v_ref, start, tk, g, heads),
                      preferred_element_type=F32) for g in hs]
        return tuple((carry[g][0] + incl[g][:, 0:1], carry[g][1] + pv[g]) for g in hs)

    init = tuple((jnp.zeros((tq, 1), F32), jnp.zeros((tq, HEAD_DIM), F32)) for _ in hs)
    carry = block(i, init, True)
    carry = lax.fori_loop(0, i, lambda t, cy: block(i - 1 - t, cy, False), carry)
    for g in range(heads):
        o_ref[:, cols[g]] = carry[g][1]


def _sb_attention(q, k, v, batch, seq, tq, tk):
    nq = seq // tq
    kv_spec = pl.BlockSpec((seq * N_SB, HEAD_DIM), lambda b, i: (b, 0))
    return pl.pallas_call(
        functools.partial(_sb_kernel, tq=tq, tk=tk, heads=N_SB),
        grid=(batch, nq),
        in_specs=[pl.BlockSpec((None, tq, QKV_BLOCK), lambda b, i: (0, b * nq + i, 0)), kv_spec, kv_spec],
        out_specs=pl.BlockSpec((tq, QKV_BLOCK), lambda b, i: (b * nq + i, 0)),
        out_shape=jax.ShapeDtypeStruct((batch * seq, QKV_BLOCK), F32),
        compiler_params=_cparams(("arbitrary", "arbitrary")),
        name="sb_attention",
    )(q, k, v)


def _fox_kernel(q_ref, k_ref, v_ref, c_ref, o_ref, *, tq, tk, heads):
    i = pl.program_id(1)
    causal = lax.broadcasted_iota(I32, (tq, tk), 1) <= lax.broadcasted_iota(I32, (tq, tk), 0)
    cols = [slice(g * HEAD_DIM, (g + 1) * HEAD_DIM) for g in range(heads)]
    q = [(q_ref[:, cs] * (SCALE * LOG2E)).astype(BF16) for cs in cols]
    q_start = pl.multiple_of(i * tq, tq)
    c_q0 = [c_ref[g, :, pl.ds(q_start, tq)][:, 0:1] for g in range(heads)]
    hs = range(heads)

    def block(j, carry, diag):
        start = pl.multiple_of(j * tk, tk)
        s = [_dot_nt(q[g], _head_rows(k_ref, start, tk, g, heads))
             + (c_q0[g] - c_ref[g, :, pl.ds(start, tk)]) * LOG2E for g in hs]
        if diag:
            s = [jnp.where(causal, x, NEG_INF) for x in s]
        m_new = [jnp.maximum(carry[g][0], jnp.max(s[g], axis=-1, keepdims=True)) for g in hs]
        p = [jnp.exp2(s[g] - m_new[g]) for g in hs]
        alpha = [jnp.exp2(carry[g][0] - m_new[g]) for g in hs]
        pv = [jnp.dot(p[g].astype(BF16), _head_rows(v_ref, start, tk, g, heads),
                      preferred_element_type=F32) for g in hs]
        return tuple((m_new[g], alpha[g] * carry[g][1] + jnp.sum(p[g], axis=-1, keepdims=True),
                      alpha[g] * carry[g][2] + pv[g]) for g in hs)

    init = tuple((jnp.full((tq, 1), NEG_INF, F32), jnp.zeros((tq, 1), F32), jnp.zeros((tq, HEAD_DIM), F32))
                 for _ in hs)
    carry = block(i, init, True)
    carry = lax.fori_loop(0, i, lambda t, cy: block(i - 1 - t, cy, False), carry)
    for g in range(heads):
        o_ref[:, cols[g]] = carry[g][2] / carry[g][1]


def _fox_attention(q, k, v, c_rows, batch, seq, tq, tk):
    nq = seq // tq
    kv_spec = pl.BlockSpec((seq * N_FOX, HEAD_DIM), lambda b, i: (b, 0))
    return pl.pallas_call(
        functools.partial(_fox_kernel, tq=tq, tk=tk, heads=N_FOX),
        grid=(batch, nq),
        in_specs=[pl.BlockSpec((None, tq, QKV_BLOCK), lambda b, i: (1, b * nq + i, 0)), kv_spec, kv_spec,
                  pl.BlockSpec((N_FOX, 1, seq), lambda b, i: (b, 0, 0))],
        out_specs=pl.BlockSpec((tq, QKV_BLOCK), lambda b, i: (b * nq + i, 0)),
        out_shape=jax.ShapeDtypeStruct((batch * seq, QKV_BLOCK), F32),
        compiler_params=_cparams(("arbitrary", "arbitrary")),
        name="fox_attention",
    )(q, k, v, c_rows)


def _reverse_cumsum_lanes(x, lane):
    n = x.shape[-1]
    d = N_SB
    while d < n:
        shifted = pltpu.roll(x, n - d, 1)
        x = x + jnp.where(lane + d < n, shifted, 0.0)
        d *= 2
    return x


def _decode_kernel(pt_ref, qs_ref, qf_ref, kown_ref, vown_ref, lfown_ref, *refs, pages_per_step):
    del pt_ref
    pp = pages_per_step
    ksb = refs[0:pp]
    vsb = refs[pp:2 * pp]
    kfx = refs[2 * pp:3 * pp]
    vfx = refs[3 * pp:4 * pp]
    lfc = refs[4 * pp:5 * pp]
    osb_ref, ofx_ref = refs[5 * pp:5 * pp + 2]
    acc_sb, acc_fx, run_sb, run_fx, m_ref, l_ref = refs[5 * pp + 2:]
    j = pl.program_id(1)
    page_rows = ksb[0].shape[0] * ksb[0].shape[1]
    rows = pp * page_rows

    qs = qs_ref[...] * SCALE
    qf = qf_ref[...] * SCALE

    @pl.when(j == 0)
    def _():
        acc_sb[...] = jnp.zeros_like(acc_sb)
        run_sb[...] = jnp.zeros_like(run_sb)
        m_ref[...] = jnp.sum(qf * kown_ref[...], axis=-1, keepdims=True)
        l_ref[...] = jnp.ones_like(l_ref)
        acc_fx[...] = vown_ref[...]
        run_fx[...] = lfown_ref[...]

    def rows_of(page_refs):
        return jnp.concatenate([page_refs[p][...].reshape(page_rows, HEAD_DIM).astype(BF16)
                                for p in reversed(range(pp))], axis=0)

    lane = lax.broadcasted_iota(I32, (N_SB, rows), 1)
    head = lax.broadcasted_iota(I32, (N_SB, rows), 0)
    valid = (lane & (N_SB - 1)) == head

    z = _dot_nt(qs.astype(BF16), rows_of(ksb))
    log_stay = jnp.where(valid, _log_sigmoid(-z), 0.0)
    incl = _reverse_cumsum_lanes(log_stay, lane)
    run = run_sb[...]
    a = jnp.where(valid, jnp.exp(z + incl + run), 0.0)
    acc_sb[...] += jnp.dot(a.astype(BF16), rows_of(vsb), preferred_element_type=F32)
    run_sb[...] = run + jnp.sum(log_stay, axis=-1, keepdims=True)

    lf = jnp.where(valid, jnp.concatenate([lfc[p][...] for p in reversed(range(pp))], axis=1), 0.0)
    incl = _reverse_cumsum_lanes(lf, lane)
    run = run_fx[...]
    s = jnp.where(valid, _dot_nt(qf.astype(BF16), rows_of(kfx)) + (incl - lf) + run, NEG_INF)
    m_old = m_ref[...]
    m_new = jnp.maximum(m_old, jnp.max(s, axis=-1, keepdims=True))
    alpha = jnp.exp(m_old - m_new)
    pr = jnp.exp(s - m_new)
    l_ref[...] = alpha * l_ref[...] + jnp.sum(pr, axis=-1, keepdims=True)
    acc_fx[...] = alpha * acc_fx[...] + jnp.dot(pr.astype(BF16), rows_of(vfx), preferred_element_type=F32)
    m_ref[...] = m_new
    run_fx[...] = run + jnp.sum(lf, axis=-1, keepdims=True)

    @pl.when(j == pl.num_programs(1) - 1)
    def _():
        osb_ref[...] = acc_sb[...]
        ofx_ref[...] = acc_fx[...] / l_ref[...]


def _decode_attention(page_table, q_sb, q_fx, k_own, v_own, lf_own, c_sb_k, c_sb_v, c_fx_k, c_fx_v,
                      c_logf_rows, pages_per_step):
    nb, n_pages = page_table.shape
    page, heads, hd = c_sb_k.shape[1:]
    rows = page * heads
    pp = pages_per_step
    steps = n_pages // pp

    def page_of(p):
        return lambda b, j, pt: (pt[b * n_pages + (n_pages - 1 - (j * pp + p))], 0, 0, 0)

    def lf_page_of(p):
        return lambda b, j, pt: (pt[b * n_pages + (n_pages - 1 - (j * pp + p))], 0, 0)

    per_b = pl.BlockSpec((None, heads, hd), lambda b, j, pt: (b, 0, 0))
    cache_specs = [pl.BlockSpec((None, page, heads, hd), page_of(p)) for p in range(pp)]
    lf_specs = [pl.BlockSpec((None, 1, rows), lf_page_of(p)) for p in range(pp)]
    grid_spec = pltpu.PrefetchScalarGridSpec(
        num_scalar_prefetch=1,
        grid=(nb, steps),
        in_specs=[per_b, per_b, per_b, per_b, pl.BlockSpec((None, heads, 1), lambda b, j, pt: (b, 0, 0))]
        + cache_specs * 4 + lf_specs,
        out_specs=[per_b, per_b],
        scratch_shapes=[pltpu.VMEM((heads, hd), F32), pltpu.VMEM((heads, hd), F32),
                        pltpu.VMEM((heads, 1), F32), pltpu.VMEM((heads, 1), F32),
                        pltpu.VMEM((heads, 1), F32), pltpu.VMEM((heads, 1), F32)],
    )
    return pl.pallas_call(
        functools.partial(_decode_kernel, pages_per_step=pp),
        grid_spec=grid_spec,
        out_shape=[jax.ShapeDtypeStruct((nb, heads, hd), F32)] * 2,
        compiler_params=_cparams(("arbitrary", "arbitrary")),
        name="decode_attention",
    )(page_table.reshape(-1), q_sb, q_fx, k_own, v_own, lf_own,
      *([c_sb_k] * pp), *([c_sb_v] * pp), *([c_fx_k] * pp), *([c_fx_v] * pp), *([c_logf_rows] * pp))


ROUTE_EXPERT, ROUTE_GATE, ROUTE_RANK = 0, 2, 4


def _merge_kernel(osb_ref, ofx_ref, x_ref, gsb_ref, gfx_ref, wout_ref, gffn_ref, wr_ref, br_ref,
                  hp_ref, u_ref, route_ref, count_ref, base_ref):
    @pl.when(pl.program_id(0) == 0)
    def _():
        base_ref[...] = jnp.zeros_like(base_ref)

    o = jnp.concatenate([_rms(osb_ref[...], gsb_ref[...]), _rms(ofx_ref[...], gfx_ref[...])], axis=-1)
    y = jnp.dot(o.astype(BF16), wout_ref[...], preferred_element_type=F32)
    hp = x_ref[...] + y
    hp_ref[...] = hp
    u = _rms(hp, gffn_ref[...])
    tm = u.shape[0]
    for s in range(SLAB_ROWS):
        u_ref[s // HALF_ROWS, pl.ds(s % HALF_ROWS, tm, stride=HALF_ROWS), :] = u[:, s * LANES:(s + 1) * LANES]

    stacked = jnp.concatenate(_split_terms(u), axis=0).astype(BF16)
    parts = jnp.dot(stacked, wr_ref[...], preferred_element_type=F32)
    both = parts[:tm] + parts[tm:]
    logits = both[:, :LANES] + both[:, LANES:] + br_ref[...]

    lane = lax.broadcasted_iota(I32, (tm, LANES), 1).astype(F32)
    is_group = (lane >= N_EXPERTS) & (lane < N_EXPERTS + N_GROUPS)
    big = float(1 << 20)
    lg = jnp.where(is_group, logits, NEG_INF)
    lg_max = jnp.max(lg, axis=-1, keepdims=True)
    gsel = jnp.min(jnp.where(lg == lg_max, lane, big), axis=-1, keepdims=True) - N_EXPERTS
    g1 = 1.0 / jnp.sum(jnp.exp(lg - lg_max), axis=-1, keepdims=True)

    in_group = (lane >= gsel * EXPERTS_PER_GROUP) & (lane < (gsel + 1) * EXPERTS_PER_GROUP)
    le = jnp.where(in_group, logits, NEG_INF)
    v_a = jnp.max(le, axis=-1, keepdims=True)
    i_a = jnp.min(jnp.where(le == v_a, lane, big), axis=-1, keepdims=True)
    le_b = jnp.where(lane == i_a, NEG_INF, le)
    v_b = jnp.max(le_b, axis=-1, keepdims=True)
    i_b = jnp.min(jnp.where(le_b == v_b, lane, big), axis=-1, keepdims=True)
    e_b = jnp.exp(v_b - v_a)
    gate_a = g1 / (1.0 + e_b)
    gate_b = g1 * e_b / (1.0 + e_b)

    onehot = jnp.where((lane == i_a) | (lane == i_b), 1.0, 0.0)
    rr = lax.broadcasted_iota(I32, (tm, tm), 0)
    cc = lax.broadcasted_iota(I32, (tm, tm), 1)
    count_dtype = BF16 if tm % 16 == 0 else F32
    earlier = (cc < rr).astype(count_dtype)
    rank_all = jnp.dot(earlier, onehot.astype(count_dtype), preferred_element_type=F32) + base_ref[...]
    rank_a = jnp.sum(jnp.where(lane == i_a, rank_all, 0.0), axis=-1, keepdims=True)
    rank_b = jnp.sum(jnp.where(lane == i_b, rank_all, 0.0), axis=-1, keepdims=True)
    total = base_ref[...] + jnp.sum(onehot, axis=0, keepdims=True)
    base_ref[...] = total
    count_ref[...] = total

    route = jnp.zeros((tm, LANES), F32)
    for k, val in ((ROUTE_EXPERT, i_a), (ROUTE_EXPERT + 1, i_b), (ROUTE_GATE, gate_a), (ROUTE_GATE + 1, gate_b),
                   (ROUTE_RANK, rank_a), (ROUTE_RANK + 1, rank_b)):
        route = jnp.where(lane == k, val, route)
    route_ref[...] = route


def _merge(o_sb, o_fx, x, g_sb, g_fx, w_out, g_ffn, w_r, b_r, tm):
    t, d = x.shape
    row = lambda n: pl.BlockSpec((tm, n), lambda i: (i, 0))
    const = lambda a: pl.BlockSpec(a.shape, lambda i: (0,) * a.ndim)
    return pl.pallas_call(
        _merge_kernel,
        grid=(t // tm,),
        in_specs=[row(o_sb.shape[1]), row(o_fx.shape[1]), row(d), const(g_sb), const(g_fx), const(w_out),
                  const(g_ffn), const(w_r), const(b_r)],
        out_specs=[row(d), pl.BlockSpec((2, tm * HALF_ROWS, LANES), lambda i: (0, i, 0)), row(LANES),
                   pl.BlockSpec((1, LANES), lambda i: (0, 0))],
        out_shape=[jax.ShapeDtypeStruct((t, d), F32), jax.ShapeDtypeStruct((2, t * HALF_ROWS, LANES), F32),
                   jax.ShapeDtypeStruct((t, LANES), F32), jax.ShapeDtypeStruct((1, LANES), F32)],
        scratch_shapes=[pltpu.VMEM((1, LANES), F32)],
        compiler_params=_cparams(("arbitrary",)),
        name="merge_router",
    )(o_sb, o_fx, x, g_sb, g_fx, w_out, g_ffn, w_r, b_r)


def _dispatch_plan(route, counts, tile, n_tiles_max):
    t = route.shape[0]
    expert = route[:, ROUTE_EXPERT:ROUTE_EXPERT + TOP_K].astype(I32)
    rank = route[:, ROUTE_RANK:ROUTE_RANK + TOP_K].astype(I32)
    count = counts[0, :N_EXPERTS].astype(I32)
    tiles = (count + tile - 1) // tile
    tile_end = jnp.cumsum(tiles)
    tile_start = tile_end - tiles
    pos = tile_start[expert] * tile + rank
    tile_ids = jnp.arange(n_tiles_max, dtype=I32)
    tile_expert = jnp.minimum(jnp.sum((tile_end[None, :] <= tile_ids[:, None]).astype(I32), axis=1), N_EXPERTS - 1)
    token = jnp.broadcast_to(jnp.arange(t, dtype=I32)[:, None], (t, TOP_K))
    row_token = jnp.zeros((n_tiles_max * tile,), I32).at[pos.reshape(-1)].set(token.reshape(-1), unique_indices=True)
    return pos.reshape(-1), tile_expert, row_token, tile_end[-1:].astype(I32)


def _slab_copy(src_hbm, src_row, dst, dst_row, sem):
    return pltpu.make_async_copy(src_hbm.at[pl.ds(pl.multiple_of(src_row * SLAB_ROWS, SLAB_ROWS), SLAB_ROWS), :],
                                 dst.at[pl.ds(pl.multiple_of(dst_row * SLAB_ROWS, SLAB_ROWS), SLAB_ROWS), :], sem)


def _rows_from_slabs(slabs, n):
    return jnp.concatenate([slabs[pl.ds(s, n, stride=SLAB_ROWS), :] for s in range(SLAB_ROWS)], axis=1)


def _expert_kernel(te_ref, rt_ref, nt_ref, u_hbm, wg_ref, wu_ref, *refs, tile, half):
    final = half == 1
    if final:
        wd_ref, part_ref, out_ref, u_vmem, xs, sem, wg_b, wu_b, wd_b = refs
    else:
        out_ref, u_vmem, xs, sem, wg_b, wu_b = refs
    i = pl.program_id(0)
    nt = nt_ref[0]
    d_e = wg_b.shape[1]

    @pl.when(i == 0)
    def _():
        load = pltpu.make_async_copy(u_hbm.at[half], u_vmem, sem.at[0])
        load.start()
        load.wait()

    @pl.when(i < nt)
    def _():
        @pl.when((i == 0) | (te_ref[i] != te_ref[jnp.maximum(i - 1, 0)]))
        def _():
            wg_b[...] = wg_ref[...].astype(BF16)
            wu_b[...] = wu_ref[...].astype(BF16)
            if final:
                wd_b[...] = wd_ref[...].astype(BF16)

        base = i * tile

        def body(r, carry):
            src = pl.multiple_of(rt_ref[base + r] * HALF_ROWS, HALF_ROWS)
            xs[pl.ds(pl.multiple_of(r * HALF_ROWS, HALF_ROWS), HALF_ROWS), :] = u_vmem[pl.ds(src, HALF_ROWS), :]
            return carry

        lax.fori_loop(0, tile, body, 0, unroll=8)
        x = jnp.concatenate([xs[pl.ds(s, tile, stride=HALF_ROWS), :] for s in range(HALF_ROWS)],
                            axis=1).astype(BF16)
        gate = jnp.dot(x, wg_b[...], preferred_element_type=F32)
        up = jnp.dot(x, wu_b[...], preferred_element_type=F32)
        if final:
            gate = gate + part_ref[:, :d_e]
            up = up + part_ref[:, d_e:]
            h = gate * (1.0 / (1.0 + jnp.exp(-gate))) * up
            y = jnp.dot(h.astype(BF16), wd_b[...], preferred_element_type=F32)
            for s in range(SLAB_ROWS):
                out_ref[pl.ds(s, tile, stride=SLAB_ROWS), :] = y[:, s * LANES:(s + 1) * LANES]
        else:
            out_ref[:, :d_e] = gate
            out_ref[:, d_e:] = up

    @pl.when(i >= nt)
    def _():
        out_ref[...] = jnp.zeros_like(out_ref)


def _experts(u_halves, tile_expert, row_token, n_tiles, w_gate, w_up, w_down, tile):
    n_e, d, d_e = w_gate.shape
    half_d = HALF_ROWS * LANES
    n_tiles_max = tile_expert.shape[0]
    last = lambda i, nt: jnp.minimum(i, nt[0] - 1)

    def call(half, extra_in, extra_specs, out_spec, out_shape, extra_scratch):
        grid_spec = pltpu.PrefetchScalarGridSpec(
            num_scalar_prefetch=3,
            grid=(n_tiles_max,),
            in_specs=[
                pl.BlockSpec(memory_space=pl.ANY),
                pl.BlockSpec((None, half_d, d_e), lambda i, te, rt, nt: (te[last(i, nt)], half, 0)),
                pl.BlockSpec((None, half_d, d_e), lambda i, te, rt, nt: (te[last(i, nt)], half, 0)),
            ] + extra_specs,
            out_specs=out_spec,
            scratch_shapes=[pltpu.VMEM(u_halves.shape[1:], F32), pltpu.VMEM((tile * HALF_ROWS, LANES), F32),
                            pltpu.SemaphoreType.DMA((1,)),
                            pltpu.VMEM((half_d, d_e), BF16), pltpu.VMEM((half_d, d_e), BF16)] + extra_scratch,
        )
        return pl.pallas_call(
            functools.partial(_expert_kernel, tile=tile, half=half),
            grid_spec=grid_spec,
            out_shape=out_shape,
            compiler_params=_cparams(("arbitrary",)),
            name=f"moe_experts_half{half}",
        )(tile_expert, row_token, n_tiles, u_halves, w_gate, w_up, *extra_in)

    part_spec = pl.BlockSpec((tile, 2 * d_e), lambda i, te, rt, nt: (i, 0))
    part = call(0, [], [], part_spec, jax.ShapeDtypeStruct((n_tiles_max * tile, 2 * d_e), F32), [])
    return call(1, [w_down, part],
                [pl.BlockSpec((None, d_e, d), lambda i, te, rt, nt: (te[last(i, nt)], 0, 0)), part_spec],
                pl.BlockSpec((tile * SLAB_ROWS, LANES), lambda i, te, rt, nt: (i, 0)),
                jax.ShapeDtypeStruct((n_tiles_max * tile * SLAB_ROWS, LANES), F32),
                [pltpu.VMEM((d_e, d), BF16)])


def _combine_kernel(pos_ref, rows_hbm, hp_ref, route_ref, gfin_ref, y_ref, buf, sem, *, tm):
    i = pl.program_id(0)
    n = pl.num_programs(0)
    slot = lax.rem(i, 2)

    def gather(tile_idx, dst_slot):
        base = tile_idx * tm * TOP_K

        def body(r, carry):
            for k in range(TOP_K):
                _slab_copy(rows_hbm, pos_ref[base + r * TOP_K + k], buf.at[dst_slot, k], r, sem.at[dst_slot]).start()
            return carry

        lax.fori_loop(0, tm, body, 0, unroll=4)

    @pl.when(i == 0)
    def _():
        gather(0, 0)

    @pl.when(i + 1 < n)
    def _():
        gather(i + 1, 1 - slot)

    pltpu.make_async_copy(buf.at[slot], buf.at[slot], sem.at[slot]).wait()
    route = route_ref[...]
    g_a = route[:, ROUTE_GATE:ROUTE_GATE + 1]
    g_b = route[:, ROUTE_GATE + 1:ROUTE_GATE + 2]
    f = g_a * _rows_from_slabs(buf.at[slot, 0], tm) + g_b * _rows_from_slabs(buf.at[slot, 1], tm)
    y_ref[...] = _rms(hp_ref[...] + f, gfin_ref[...])


def _combine(pos, rows, hp, route, g_final, tm):
    t, d = hp.shape
    grid_spec = pltpu.PrefetchScalarGridSpec(
        num_scalar_prefetch=1,
        grid=(t // tm,),
        in_specs=[
            pl.BlockSpec(memory_space=pl.ANY),
            pl.BlockSpec((tm, d), lambda i, pos: (i, 0)),
            pl.BlockSpec((tm, LANES), lambda i, pos: (i, 0)),
            pl.BlockSpec((1, d), lambda i, pos: (0, 0)),
        ],
        out_specs=pl.BlockSpec((tm, d), lambda i, pos: (i, 0)),
        scratch_shapes=[pltpu.VMEM((2, TOP_K, tm * SLAB_ROWS, LANES), F32), pltpu.SemaphoreType.DMA((2,))],
    )
    return pl.pallas_call(
        functools.partial(_combine_kernel, tm=tm),
        grid_spec=grid_spec,
        out_shape=jax.ShapeDtypeStruct((t, d), F32),
        compiler_params=_cparams(("arbitrary",)),
        name="moe_combine",
    )(pos, rows, hp, route, g_final)


def _routed_moe(u, hp, route, counts, w_gate, w_up, w_down, g_final, tile, tm_combine):
    t = hp.shape[0]
    n_tiles_max = (TOP_K * t + N_EXPERTS * (tile - 1)) // tile
    pos, tile_expert, row_token, n_tiles = _dispatch_plan(route, counts, tile, n_tiles_max)
    rows = _experts(u, tile_expert, row_token, n_tiles, w_gate, w_up, w_down, tile)
    return _combine(pos, rows, hp, route, g_final, tm_combine)


def _pad_lanes(a, n=LANES):
    return jnp.pad(a, ((0, 0), (0, n - a.shape[1])))


def kernel(x_prompt, x_sample, cache_sb_k, cache_sb_v, cache_fox_k, cache_fox_v, cache_fox_logf, page_table, norm_attn_g, w_in, b_forget, g_sb_out, g_fox_out, w_out, norm_ffn_g, w_router_group, b_router_group, w_router_expert, b_router_expert, w_expert_gate, w_expert_up, w_expert_down, norm_final_g):
    batch, seq, d = x_prompt.shape
    nb = x_sample.shape[0]
    assert w_in.shape[0] == 1, "one trunk layer"
    d_qkv = 3 * (N_SB + N_FOX) * HEAD_DIM

    w_main = w_in[0, :, :d_qkv].astype(BF16)
    w_f = _pad_lanes(w_in[0, :, d_qkv:]).astype(BF16)
    b_f = _pad_lanes(b_forget)
    g_attn = norm_attn_g
    w_out_b = w_out[0].astype(BF16)
    w_r = jnp.concatenate([w_router_expert[0].transpose(1, 0, 2).reshape(d, N_EXPERTS), w_router_group[0]], axis=1)
    w_r = _pad_lanes(w_r)
    w_r_hi = w_r.astype(BF16)
    w_r_lo = (w_r - w_r_hi.astype(F32)).astype(BF16)
    w_r2 = jnp.concatenate([w_r_hi, w_r_lo], axis=1)
    b_r = _pad_lanes(jnp.concatenate([b_router_expert[0].reshape(1, N_EXPERTS), b_router_group], axis=1))
    g_fin = norm_final_g.reshape(1, d)
    experts = (w_expert_gate[0], w_expert_up[0], w_expert_down[0])

    xp = x_prompt.reshape(batch * seq, d)
    q_p, logf_p, *kv_p = _inproj(xp, g_attn, w_main, w_f, b_f, tm=TM_INPROJ)
    logf_p = logf_p[:, :N_FOX].reshape(batch, seq, N_FOX)
    c_rows = _cumsum_rows(logf_p.transpose(0, 2, 1)).reshape(batch * N_FOX, 1, seq)
    o_sb = _sb_attention(q_p, kv_p[0], kv_p[1], batch, seq, TQ, TK)
    o_fx = _fox_attention(q_p, kv_p[2], kv_p[3], c_rows, batch, seq, TQ, TK)
    hp_p, u_p, route_p, count_p = _merge(o_sb, o_fx, xp, g_sb_out, g_fox_out, w_out_b, norm_ffn_g, w_r2, b_r,
                                         tm=TM_MERGE)
    y_prompt = _routed_moe(u_p, hp_p, route_p, count_p, *experts, g_fin, TILE_EXPERT, TM_COMBINE)
    y_prompt = y_prompt.reshape(batch, seq, d)

    xs = x_sample.reshape(nb, d)
    q_s, logf_s, *kv_s = _inproj(xs, g_attn, w_main, w_f, b_f, tm=nb)
    logf_s = logf_s[:, :N_FOX]
    per_head = lambda a: a.reshape(nb, N_SB, HEAD_DIM)
    page = cache_fox_logf.shape[2]
    logf_rows = cache_fox_logf[0].reshape(-1, 1, page * N_FOX)
    o_sb_s, o_fx_s = _decode_attention(
        page_table, per_head(q_s[0]), per_head(q_s[1]), per_head(kv_s[2]), per_head(kv_s[3]),
        logf_s.reshape(nb, N_FOX, 1), cache_sb_k[0], cache_sb_v[0], cache_fox_k[0], cache_fox_v[0],
        logf_rows, PAGES_PER_STEP)
    hp_s, u_s, route_s, count_s = _merge(o_sb_s.reshape(nb, -1), o_fx_s.reshape(nb, -1), xs, g_sb_out, g_fox_out,
                                         w_out_b, norm_ffn_g, w_r2, b_r, tm=nb)
    y_sample = _routed_moe(u_s, hp_s, route_s, count_s, *experts, g_fin, nb, nb).reshape(nb, 1, d)

    heads_p = lambda a: a.reshape(1, batch, seq, N_SB, HEAD_DIM)
    heads_s = lambda a: a.reshape(1, nb, 1, N_SB, HEAD_DIM)
    return (y_prompt, y_sample,
            heads_p(kv_p[0]), heads_p(kv_p[1]), heads_p(kv_p[2]), heads_p(kv_p[3]), logf_p[None],
            heads_s(kv_s[0]), heads_s(kv_s[1]), heads_s(kv_s[2]), heads_s(kv_s[3]),
            logf_s.reshape(1, nb, 1, N_FOX))
```

```python
import functools

import jax
import jax.numpy as jnp
from jax import lax
from jax.experimental import pallas as pl
from jax.experimental.pallas import tpu as pltpu

HEAD_DIM = 128
N_SB = 8
N_FOX = 8
N_GROUPS = 4
EXPERTS_PER_GROUP = 8
N_EXPERTS = N_GROUPS * EXPERTS_PER_GROUP
TOP_K = 2
SCALE = HEAD_DIM ** -0.5
LOG2E = 1.4426950408889634
RMS_EPS = 1e-6
LANES = 128
D_MODEL = 2048
QKV_BLOCK = N_SB * HEAD_DIM
SLAB_ROWS = D_MODEL // LANES
HALF_ROWS = SLAB_ROWS // 2
VMEM_LIMIT = 56 * 1024 * 1024
NEG_INF = float("-inf")
BF16 = jnp.bfloat16
F32 = jnp.float32
I32 = jnp.int32

TM_INPROJ = 512
TQ = 256
TK = 256
HEADS_PER_STEP = 8
PAGES_PER_STEP = 8
TM_MERGE = 512
TILE_EXPERT = 256
TM_COMBINE = 256


def _cparams(sem):
    return pltpu.CompilerParams(dimension_semantics=sem, vmem_limit_bytes=VMEM_LIMIT)


def _rms(x, g):
    return x * lax.rsqrt(jnp.mean(x * x, axis=-1, keepdims=True) + RMS_EPS) * g


def _log_sigmoid(x):
    return jnp.minimum(x, 0.0) - jnp.log(1.0 + jnp.exp(-jnp.abs(x)))


def _split_terms(x, terms=2):
    pieces = []
    rest = x
    for _ in range(terms - 1):
        head = rest.astype(BF16).astype(F32)
        pieces.append(head)
        rest = rest - head
    pieces.append(rest)
    return pieces


def _log2_sigmoid_neg(z2):
    neg_abs = pltpu.bitcast(pltpu.bitcast(z2, jnp.uint32) | jnp.uint32(0x80000000), F32)
    return jnp.log(1.0 + jnp.exp2(neg_abs)) * (-LOG2E) - jnp.maximum(z2, 0.0)


def _split_hi_lo_lanes(x):
    hi = pltpu.bitcast(pltpu.bitcast(x, jnp.uint32) & jnp.uint32(0xFFFF0000), F32)
    return jnp.concatenate([hi, x - hi], axis=1).astype(BF16)


def _dot_nt(a, b):
    return lax.dot_general(a, b, (((1,), (1,)), ((), ())), preferred_element_type=F32)


def _inproj_kernel(x_ref, g_ref, w_ref, wf_ref, bf_ref, z_ref, logf_ref, ksb_ref, vsb_ref, kfx_ref, vfx_ref,
                   u_ref):
    j = pl.program_id(1)
    tm = x_ref.shape[0]

    @pl.when(j == 0)
    def _():
        u = _rms(x_ref[...], g_ref[...]).astype(BF16)
        u_ref[...] = u
        f = jnp.dot(u, wf_ref[...], preferred_element_type=F32) + bf_ref[...]
        logf_ref[...] = _log_sigmoid(f)

    z_ref[...] = jnp.dot(u_ref[...], w_ref[...], preferred_element_type=F32)

    for block, ref in ((1, ksb_ref), (2, vsb_ref), (4, kfx_ref), (5, vfx_ref)):
        @pl.when(j == block)
        def _(ref=ref):
            for h in range(N_SB):
                ref[pl.ds(h, tm, stride=N_SB), :] = z_ref[:, h * HEAD_DIM:(h + 1) * HEAD_DIM]


def _inproj(x, g, w_main, w_f, b_f, tm):
    t, d = x.shape
    n_blocks = w_main.shape[1] // QKV_BLOCK
    kv_spec = pl.BlockSpec((tm * N_SB, HEAD_DIM), lambda i, j: (i, 0))
    kv_shape = jax.ShapeDtypeStruct((t * N_SB, HEAD_DIM), F32)
    return pl.pallas_call(
        _inproj_kernel,
        grid=(t // tm, n_blocks),
        in_specs=[
            pl.BlockSpec((tm, d), lambda i, j: (i, 0)),
            pl.BlockSpec((1, d), lambda i, j: (0, 0)),
            pl.BlockSpec((d, QKV_BLOCK), lambda i, j: (0, j)),
            pl.BlockSpec((d, LANES), lambda i, j: (0, 0)),
            pl.BlockSpec((1, LANES), lambda i, j: (0, 0)),
        ],
        out_specs=[
            pl.BlockSpec((None, tm, QKV_BLOCK), lambda i, j: (j, i, 0)),
            pl.BlockSpec((tm, LANES), lambda i, j: (i, 0)),
            kv_spec, kv_spec, kv_spec, kv_spec,
        ],
        out_shape=[
            jax.ShapeDtypeStruct((n_blocks, t, QKV_BLOCK), F32),
            jax.ShapeDtypeStruct((t, LANES), F32),
            kv_shape, kv_shape, kv_shape, kv_shape,
        ],
        scratch_shapes=[pltpu.VMEM((tm, d), BF16)],
        compiler_params=_cparams(("arbitrary", "arbitrary")),
        name="inproj",
    )(x, g, w_main, w_f, b_f)


def _cumsum_kernel(x_ref, c_ref, *, chunk):
    s = x_ref.shape[-1]
    r = lax.broadcasted_iota(I32, (chunk, chunk), 0)
    c = lax.broadcasted_iota(I32, (chunk, chunk), 1)
    upper = (r <= c).astype(BF16)
    carry = jnp.zeros((x_ref.shape[0], 1), F32)
    for k in range(s // chunk):
        x = x_ref[:, k * chunk:(k + 1) * chunk]
        stacked = jnp.concatenate(_split_terms(x, 3), axis=0).astype(BF16)
        parts = jnp.dot(stacked, upper, preferred_element_type=F32)
        n = x.shape[0]
        cs = parts[:n] + parts[n:2 * n] + parts[2 * n:] + carry
        c_ref[:, k * chunk:(k + 1) * chunk] = cs
        carry = cs[:, chunk - 1:chunk]


def _cumsum_rows(x):
    b, h, s = x.shape
    return pl.pallas_call(
        functools.partial(_cumsum_kernel, chunk=256),
        grid=(b,),
        in_specs=[pl.BlockSpec((None, h, s), lambda i: (i, 0, 0))],
        out_specs=pl.BlockSpec((None, h, s), lambda i: (i, 0, 0)),
        out_shape=jax.ShapeDtypeStruct((b, h, s), F32),
        compiler_params=_cparams(("arbitrary",)),
        name="logf_cumsum",
    )(x)


def _head_rows(ref, start, n, head, heads):
    del heads
    return ref[pl.ds(start, n), head * HEAD_DIM:(head + 1) * HEAD_DIM].astype(BF16)


def _sb_kernel(q_ref, k_ref, v_ref, o_ref, *, tq, tk, heads):
    i = pl.program_id(1)
    r = lax.broadcasted_iota(I32, (2 * tk, tk), 0)
    c = lax.broadcasted_iota(I32, (2 * tk, tk), 1)
    not_before2 = ((r & (tk - 1)) >= c).astype(BF16)
    strictly_causal = lax.broadcasted_iota(I32, (tq, tk), 1) < lax.broadcasted_iota(I32, (tq, tk), 0)
    cols = [slice(g * HEAD_DIM, (g + 1) * HEAD_DIM) for g in range(heads)]
    q = [(q_ref[:, cs] * (SCALE * LOG2E)).astype(BF16) for cs in cols]
    hs = range(heads)

    def block(j, carry, diag):
        start = pl.multiple_of(j * tk, tk)
        z = [_dot_nt(q[g], _head_rows(k_ref, start, tk, g, heads)) for g in hs]
        log_stay = [_log2_sigmoid_neg(z[g]) for g in hs]
        if diag:
            log_stay = [jnp.where(strictly_causal, x, 0.0) for x in log_stay]
        incl = [jnp.dot(_split_hi_lo_lanes(log_stay[g]), not_before2, preferred_element_type=F32) for g in hs]
        a = [jnp.exp2(z[g] + incl[g] + carry[g][0]) for g in hs]
        if diag:
            a = [jnp.where(strictly_causal, x, 0.0) for x in a]
        pv = [jnp.dot(a[g].astype(BF16), _head_rows(v_ref, start, tk, g, heads),
                      preferred_element_type=F32) for g in hs]
        return tuple((carry[g][0] + incl[g][:, 0:1], carry[g][1] + pv[g]) for g in hs)

    init = tuple((jnp.zeros((tq, 1), F32), jnp.zeros((tq, HEAD_DIM), F32)) for _ in hs)
    carry = block(i, init, True)
    carry = lax.fori_loop(0, i, lambda t, cy: block(i - 1 - t, cy, False), carry)
    for g in range(heads):
        o_ref[:, cols[g]] = carry[g][1]


def _sb_attention(z, batch, seq, tq, tk):
    nq = seq // tq
    kv_spec = lambda which: pl.BlockSpec((None, seq, QKV_BLOCK), lambda b, i: (which, b, 0))
    return pl.pallas_call(
        functools.partial(_sb_kernel, tq=tq, tk=tk, heads=N_SB),
        grid=(batch, nq),
        in_specs=[pl.BlockSpec((None, tq, QKV_BLOCK), lambda b, i: (0, b * nq + i, 0)), kv_spec(1), kv_spec(2)],
        out_specs=pl.BlockSpec((tq, QKV_BLOCK), lambda b, i: (b * nq + i, 0)),
        out_shape=jax.ShapeDtypeStruct((batch * seq, QKV_BLOCK), F32),
        compiler_params=_cparams(("arbitrary", "arbitrary")),
        name="sb_attention",
    )(z, z, z)


def _fox_kernel(q_ref, k_ref, v_ref, c_ref, o_ref, *, tq, tk, heads):
    i = pl.program_id(1)
    causal = lax.broadcasted_iota(I32, (tq, tk), 1) <= lax.broadcasted_iota(I32, (tq, tk), 0)
    cols = [slice(g * HEAD_DIM, (g + 1) * HEAD_DIM) for g in range(heads)]
    q = [(q_ref[:, cs] * (SCALE * LOG2E)).astype(BF16) for cs in cols]
    q_start = pl.multiple_of(i * tq, tq)
    c_q0 = [c_ref[g, :, pl.ds(q_start, tq)][:, 0:1] for g in range(heads)]
    hs = range(heads)

    def block(j, carry, diag):
        start = pl.multiple_of(j * tk, tk)
        s = [_dot_nt(q[g], _head_rows(k_ref, start, tk, g, heads))
             + (c_q0[g] - c_ref[g, :, pl.ds(start, tk)]) * LOG2E for g in hs]
        if diag:
            s = [jnp.where(causal, x, NEG_INF) for x in s]
        m_new = [jnp.maximum(carry[g][0], jnp.max(s[g], axis=-1, keepdims=True)) for g in hs]
        p = [jnp.exp2(s[g] - m_new[g]) for g in hs]
        alpha = [jnp.exp2(carry[g][0] - m_new[g]) for g in hs]
        pv = [jnp.dot(p[g].astype(BF16), _head_rows(v_ref, start, tk, g, heads),
                      preferred_element_type=F32) for g in hs]
        return tuple((m_new[g], alpha[g] * carry[g][1] + jnp.sum(p[g], axis=-1, keepdims=True),
                      alpha[g] * carry[g][2] + pv[g]) for g in hs)

    init = tuple((jnp.full((tq, 1), NEG_INF, F32), jnp.zeros((tq, 1), F32), jnp.zeros((tq, HEAD_DIM), F32))
                 for _ in hs)
    carry = block(i, init, True)
    carry = lax.fori_loop(0, i, lambda t, cy: block(i - 1 - t, cy, False), carry)
    for g in range(heads):
        o_ref[:, cols[g]] = carry[g][2] / carry[g][1]


def _fox_attention(z, c_rows, batch, seq, tq, tk):
    nq = seq // tq
    kv_spec = lambda which: pl.BlockSpec((None, seq, QKV_BLOCK), lambda b, i: (which, b, 0))
    return pl.pallas_call(
        functools.partial(_fox_kernel, tq=tq, tk=tk, heads=N_FOX),
        grid=(batch, nq),
        in_specs=[pl.BlockSpec((None, tq, QKV_BLOCK), lambda b, i: (3, b * nq + i, 0)), kv_spec(4), kv_spec(5),
                  pl.BlockSpec((N_FOX, 1, seq), lambda b, i: (b, 0, 0))],
        out_specs=pl.BlockSpec((tq, QKV_BLOCK), lambda b, i: (b * nq + i, 0)),
        out_shape=jax.ShapeDtypeStruct((batch * seq, QKV_BLOCK), F32),
        compiler_params=_cparams(("arbitrary", "arbitrary")),
        name="fox_attention",
    )(z, z, z, c_rows)


def _reverse_cumsum_lanes(x, lane):
    n = x.shape[-1]
    d = N_SB
    while d < n:
        shifted = pltpu.roll(x, n - d, 1)
        x = x + jnp.where(lane + d < n, shifted, 0.0)
        d *= 2
    return x


def _decode_kernel(pt_ref, qs_ref, qf_ref, kown_ref, vown_ref, lfown_ref, *refs, pages_per_step):
    del pt_ref
    pp = pages_per_step
    ksb = refs[0:pp]
    vsb = refs[pp:2 * pp]
    kfx = refs[2 * pp:3 * pp]
    vfx = refs[3 * pp:4 * pp]
    lfc = refs[4 * pp:5 * pp]
    osb_ref, ofx_ref = refs[5 * pp:5 * pp + 2]
    acc_sb, acc_fx, run_sb, run_fx, m_ref, l_ref = refs[5 * pp + 2:]
    j = pl.program_id(1)
    page_rows = ksb[0].shape[0] * ksb[0].shape[1]
    rows = pp * page_rows

    qs = qs_ref[...] * SCALE
    qf = qf_ref[...] * SCALE

    @pl.when(j == 0)
    def _():
        acc_sb[...] = jnp.zeros_like(acc_sb)
        run_sb[...] = jnp.zeros_like(run_sb)
        m_ref[...] = jnp.sum(qf * kown_ref[...], axis=-1, keepdims=True)
        l_ref[...] = jnp.ones_like(l_ref)
        acc_fx[...] = vown_ref[...]
        run_fx[...] = lfown_ref[...]

    def rows_of(page_refs):
        return jnp.concatenate([page_refs[p][...].reshape(page_rows, HEAD_DIM).astype(BF16)
                                for p in reversed(range(pp))], axis=0)

    lane = lax.broadcasted_iota(I32, (N_SB, rows), 1)
    head = lax.broadcasted_iota(I32, (N_SB, rows), 0)
    valid = (lane & (N_SB - 1)) == head

    z = _dot_nt(qs.astype(BF16), rows_of(ksb))
    log_stay = jnp.where(valid, _log_sigmoid(-z), 0.0)
    incl = _reverse_cumsum_lanes(log_stay, lane)
    run = run_sb[...]
    a = jnp.where(valid, jnp.exp(z + incl + run), 0.0)
    acc_sb[...] += jnp.dot(a.astype(BF16), rows_of(vsb), preferred_element_type=F32)
    run_sb[...] = run + jnp.sum(log_stay, axis=-1, keepdims=True)

    lf = jnp.where(valid, jnp.concatenate([lfc[p][...] for p in reversed(range(pp))], axis=1), 0.0)
    incl = _reverse_cumsum_lanes(lf, lane)
    run = run_fx[...]
    s = jnp.where(valid, _dot_nt(qf.astype(BF16), rows_of(kfx)) + (incl - lf) + run, NEG_INF)
    m_old = m_ref[...]
    m_new = jnp.maximum(m_old, jnp.max(s, axis=-1, keepdims=True))
    alpha = jnp.exp(m_old - m_new)
    pr = jnp.exp(s - m_new)
    l_ref[...] = alpha * l_ref[...] + jnp.sum(pr, axis=-1, keepdims=True)
    acc_fx[...] = alpha * acc_fx[...] + jnp.dot(pr.astype(BF16), rows_of(vfx), preferred_element_type=F32)
    m_ref[...] = m_new
    run_fx[...] = run + jnp.sum(lf, axis=-1, keepdims=True)

    @pl.when(j == pl.num_programs(1) - 1)
    def _():
        osb_ref[...] = acc_sb[...]
        ofx_ref[...] = acc_fx[...] / l_ref[...]


def _decode_attention(page_table, q_sb, q_fx, k_own, v_own, lf_own, c_sb_k, c_sb_v, c_fx_k, c_fx_v,
                      c_logf_rows, pages_per_step):
    nb, n_pages = page_table.shape
    page, heads, hd = c_sb_k.shape[1:]
    rows = page * heads
    pp = pages_per_step
    steps = n_pages // pp

    def page_of(p):
        return lambda b, j, pt: (pt[b * n_pages + (n_pages - 1 - (j * pp + p))], 0, 0, 0)

    def lf_page_of(p):
        return lambda b, j, pt: (pt[b * n_pages + (n_pages - 1 - (j * pp + p))], 0, 0)

    per_b = pl.BlockSpec((None, heads, hd), lambda b, j, pt: (b, 0, 0))
    cache_specs = [pl.BlockSpec((None, page, heads, hd), page_of(p)) for p in range(pp)]
    lf_specs = [pl.BlockSpec((None, 1, rows), lf_page_of(p)) for p in range(pp)]
    grid_spec = pltpu.PrefetchScalarGridSpec(
        num_scalar_prefetch=1,
        grid=(nb, steps),
        in_specs=[per_b, per_b, per_b, per_b, pl.BlockSpec((None, heads, 1), lambda b, j, pt: (b, 0, 0))]
        + cache_specs * 4 + lf_specs,
        out_specs=[per_b, per_b],
        scratch_shapes=[pltpu.VMEM((heads, hd), F32), pltpu.VMEM((heads, hd), F32),
                        pltpu.VMEM((heads, 1), F32), pltpu.VMEM((heads, 1), F32),
                        pltpu.VMEM((heads, 1), F32), pltpu.VMEM((heads, 1), F32)],
    )
    return pl.pallas_call(
        functools.partial(_decode_kernel, pages_per_step=pp),
        grid_spec=grid_spec,
        out_shape=[jax.ShapeDtypeStruct((nb, heads, hd), F32)] * 2,
        compiler_params=_cparams(("arbitrary", "arbitrary")),
        name="decode_attention",
    )(page_table.reshape(-1), q_sb, q_fx, k_own, v_own, lf_own,
      *([c_sb_k] * pp), *([c_sb_v] * pp), *([c_fx_k] * pp), *([c_fx_v] * pp), *([c_logf_rows] * pp))


ROUTE_EXPERT, ROUTE_GATE, ROUTE_RANK = 0, 2, 4


def _merge_kernel(osb_ref, ofx_ref, x_ref, gsb_ref, gfx_ref, wout_ref, gffn_ref, wr_ref, br_ref,
                  hp_ref, u_ref, route_ref, count_ref, base_ref):
    @pl.when(pl.program_id(0) == 0)
    def _():
        base_ref[...] = jnp.zeros_like(base_ref)

    o = jnp.concatenate([_rms(osb_ref[...], gsb_ref[...]), _rms(ofx_ref[...], gfx_ref[...])], axis=-1)
    y = jnp.dot(o.astype(BF16), wout_ref[...], preferred_element_type=F32)
    hp = x_ref[...] + y
    hp_ref[...] = hp
    u = _rms(hp, gffn_ref[...])
    tm = u.shape[0]
    for s in range(SLAB_ROWS):
        u_ref[s // HALF_ROWS, pl.ds(s % HALF_ROWS, tm, stride=HALF_ROWS), :] = u[:, s * LANES:(s + 1) * LANES]

    stacked = jnp.concatenate(_split_terms(u), axis=0).astype(BF16)
    parts = jnp.dot(stacked, wr_ref[...], preferred_element_type=F32)
    both = parts[:tm] + parts[tm:]
    logits = both[:, :LANES] + both[:, LANES:] + br_ref[...]

    lane = lax.broadcasted_iota(I32, (tm, LANES), 1).astype(F32)
    is_group = (lane >= N_EXPERTS) & (lane < N_EXPERTS + N_GROUPS)
    big = float(1 << 20)
    lg = jnp.where(is_group, logits, NEG_INF)
    lg_max = jnp.max(lg, axis=-1, keepdims=True)
    gsel = jnp.min(jnp.where(lg == lg_max, lane, big), axis=-1, keepdims=True) - N_EXPERTS
    g1 = 1.0 / jnp.sum(jnp.exp(lg - lg_max), axis=-1, keepdims=True)

    in_group = (lane >= gsel * EXPERTS_PER_GROUP) & (lane < (gsel + 1) * EXPERTS_PER_GROUP)
    le = jnp.where(in_group, logits, NEG_INF)
    v_a = jnp.max(le, axis=-1, keepdims=True)
    i_a = jnp.min(jnp.where(le == v_a, lane, big), axis=-1, keepdims=True)
    le_b = jnp.where(lane == i_a, NEG_INF, le)
    v_b = jnp.max(le_b, axis=-1, keepdims=True)
    i_b = jnp.min(jnp.where(le_b == v_b, lane, big), axis=-1, keepdims=True)
    e_b = jnp.exp(v_b - v_a)
    gate_a = g1 / (1.0 + e_b)
    gate_b = g1 * e_b / (1.0 + e_b)

    onehot = jnp.where((lane == i_a) | (lane == i_b), 1.0, 0.0)
    rr = lax.broadcasted_iota(I32, (tm, tm), 0)
    cc = lax.broadcasted_iota(I32, (tm, tm), 1)
    count_dtype = BF16 if tm % 16 == 0 else F32
    earlier = (cc < rr).astype(count_dtype)
    rank_all = jnp.dot(earlier, onehot.astype(count_dtype), preferred_element_type=F32) + base_ref[...]
    rank_a = jnp.sum(jnp.where(lane == i_a, rank_all, 0.0), axis=-1, keepdims=True)
    rank_b = jnp.sum(jnp.where(lane == i_b, rank_all, 0.0), axis=-1, keepdims=True)
    total = base_ref[...] + jnp.sum(onehot, axis=0, keepdims=True)
    base_ref[...] = total
    count_ref[...] = total

    route = jnp.zeros((tm, LANES), F32)
    for k, val in ((ROUTE_EXPERT, i_a), (ROUTE_EXPERT + 1, i_b), (ROUTE_GATE, gate_a), (ROUTE_GATE + 1, gate_b),
                   (ROUTE_RANK, rank_a), (ROUTE_RANK + 1, rank_b)):
        route = jnp.where(lane == k, val, route)
    route_ref[...] = route


def _merge(o_sb, o_fx, x, g_sb, g_fx, w_out, g_ffn, w_r, b_r, tm):
    t, d = x.shape
    row = lambda n: pl.BlockSpec((tm, n), lambda i: (i, 0))
    const = lambda a: pl.BlockSpec(a.shape, lambda i: (0,) * a.ndim)
    return pl.pallas_call(
        _merge_kernel,
        grid=(t // tm,),
        in_specs=[row(o_sb.shape[1]), row(o_fx.shape[1]), row(d), const(g_sb), const(g_fx), const(w_out),
                  const(g_ffn), const(w_r), const(b_r)],
        out_specs=[row(d), pl.BlockSpec((2, tm * HALF_ROWS, LANES), lambda i: (0, i, 0)), row(LANES),
                   pl.BlockSpec((1, LANES), lambda i: (0, 0))],
        out_shape=[jax.ShapeDtypeStruct((t, d), F32), jax.ShapeDtypeStruct((2, t * HALF_ROWS, LANES), F32),
                   jax.ShapeDtypeStruct((t, LANES), F32), jax.ShapeDtypeStruct((1, LANES), F32)],
        scratch_shapes=[pltpu.VMEM((1, LANES), F32)],
        compiler_params=_cparams(("arbitrary",)),
        name="merge_router",
    )(o_sb, o_fx, x, g_sb, g_fx, w_out, g_ffn, w_r, b_r)


def _dispatch_plan(route, counts, tile, n_tiles_max):
    t = route.shape[0]
    expert = route[:, ROUTE_EXPERT:ROUTE_EXPERT + TOP_K].T.astype(I32)
    rank = route[:, ROUTE_RANK:ROUTE_RANK + TOP_K].T.astype(I32)
    count = counts[0, :N_EXPERTS].astype(I32)
    tiles = (count + tile - 1) // tile
    tile_end = jnp.cumsum(tiles)
    first_row = (tile_end - tiles) * tile
    pos = (jnp.take(first_row, expert) + rank).reshape(-1)
    tile_ids = jnp.arange(n_tiles_max, dtype=I32)
    tile_expert = jnp.minimum(jnp.sum((tile_end[None, :] <= tile_ids[:, None]).astype(I32), axis=1), N_EXPERTS - 1)
    token = jnp.tile(jnp.arange(t, dtype=I32), TOP_K)
    row_token = jnp.zeros((n_tiles_max * tile,), I32).at[pos].set(token, unique_indices=True)
    return pos, tile_expert, row_token, tile_end[-1:].astype(I32)


def _slab_copy(src_hbm, src_row, dst, dst_row, sem):
    return pltpu.make_async_copy(src_hbm.at[pl.ds(pl.multiple_of(src_row * SLAB_ROWS, SLAB_ROWS), SLAB_ROWS), :],
                                 dst.at[pl.ds(pl.multiple_of(dst_row * SLAB_ROWS, SLAB_ROWS), SLAB_ROWS), :], sem)


def _rows_from_slabs(slabs, n):
    return jnp.concatenate([slabs[pl.ds(s, n, stride=SLAB_ROWS), :] for s in range(SLAB_ROWS)], axis=1)


def _expert_kernel(te_ref, rt_ref, nt_ref, u_hbm, wg_ref, wu_ref, *refs, tile, half):
    final = half == 1
    if final:
        wd_ref, part_ref, out_ref, u_vmem, xs, sem, wg_b, wu_b, wd_b = refs
    else:
        out_ref, u_vmem, xs, sem, wg_b, wu_b = refs
    i = pl.program_id(0)
    nt = nt_ref[0]
    d_e = wg_b.shape[1]

    @pl.when(i == 0)
    def _():
        load = pltpu.make_async_copy(u_hbm.at[half], u_vmem, sem.at[0])
        load.start()
        load.wait()

    @pl.when(i < nt)
    def _():
        @pl.when((i == 0) | (te_ref[i] != te_ref[jnp.maximum(i - 1, 0)]))
        def _():
            wg_b[...] = wg_ref[...].astype(BF16)
            wu_b[...] = wu_ref[...].astype(BF16)
            if final:
                wd_b[...] = wd_ref[...].astype(BF16)

        base = i * tile

        def body(r, carry):
            src = pl.multiple_of(rt_ref[base + r] * HALF_ROWS, HALF_ROWS)
            xs[pl.ds(pl.multiple_of(r * HALF_ROWS, HALF_ROWS), HALF_ROWS), :] = u_vmem[pl.ds(src, HALF_ROWS), :]
            return carry

        lax.fori_loop(0, tile, body, 0, unroll=8)
        x = jnp.concatenate([xs[pl.ds(s, tile, stride=HALF_ROWS), :] for s in range(HALF_ROWS)],
                            axis=1).astype(BF16)
        gate = jnp.dot(x, wg_b[...], preferred_element_type=F32)
        up = jnp.dot(x, wu_b[...], preferred_element_type=F32)
        if final:
            gate = gate + part_ref[:, :d_e]
            up = up + part_ref[:, d_e:]
            h = gate * (1.0 / (1.0 + jnp.exp(-gate))) * up
            y = jnp.dot(h.astype(BF16), wd_b[...], preferred_element_type=F32)
            for s in range(SLAB_ROWS):
                out_ref[pl.ds(s, tile, stride=SLAB_ROWS), :] = y[:, s * LANES:(s + 1) * LANES]
        else:
            out_ref[:, :d_e] = gate
            out_ref[:, d_e:] = up

    @pl.when(i >= nt)
    def _():
        out_ref[...] = jnp.zeros_like(out_ref)


def _experts(u_halves, tile_expert, row_token, n_tiles, w_gate, w_up, w_down, tile):
    n_e, d, d_e = w_gate.shape
    half_d = HALF_ROWS * LANES
    n_tiles_max = tile_expert.shape[0]
    last = lambda i, nt: jnp.minimum(i, nt[0] - 1)

    def call(half, extra_in, extra_specs, out_spec, out_shape, extra_scratch):
        grid_spec = pltpu.PrefetchScalarGridSpec(
            num_scalar_prefetch=3,
            grid=(n_tiles_max,),
            in_specs=[
                pl.BlockSpec(memory_space=pl.ANY),
                pl.BlockSpec((None, half_d, d_e), lambda i, te, rt, nt: (te[last(i, nt)], half, 0)),
                pl.BlockSpec((None, half_d, d_e), lambda i, te, rt, nt: (te[last(i, nt)], half, 0)),
            ] + extra_specs,
            out_specs=out_spec,
            scratch_shapes=[pltpu.VMEM(u_halves.shape[1:], F32), pltpu.VMEM((tile * HALF_ROWS, LANES), F32),
                            pltpu.SemaphoreType.DMA((1,)),
                            pltpu.VMEM((half_d, d_e), BF16), pltpu.VMEM((half_d, d_e), BF16)] + extra_scratch,
        )
        return pl.pallas_call(
            functools.partial(_expert_kernel, tile=tile, half=half),
            grid_spec=grid_spec,
            out_shape=out_shape,
            compiler_params=_cparams(("arbitrary",)),
            name=f"moe_experts_half{half}",
        )(tile_expert, row_token, n_tiles, u_halves, w_gate, w_up, *extra_in)

    part_spec = pl.BlockSpec((tile, 2 * d_e), lambda i, te, rt, nt: (i, 0))
    part = call(0, [], [], part_spec, jax.ShapeDtypeStruct((n_tiles_max * tile, 2 * d_e), F32), [])
    return call(1, [w_down, part],
                [pl.BlockSpec((None, d_e, d), lambda i, te, rt, nt: (te[last(i, nt)], 0, 0)), part_spec],
                pl.BlockSpec((tile * SLAB_ROWS, LANES), lambda i, te, rt, nt: (i, 0)),
                jax.ShapeDtypeStruct((n_tiles_max * tile * SLAB_ROWS, LANES), F32),
                [pltpu.VMEM((d_e, d), BF16)])


def _combine_kernel(pos_ref, rows_hbm, hp_ref, route_ref, gfin_ref, y_ref, buf, sem, *, tm):
    i = pl.program_id(0)
    n = pl.num_programs(0)
    slot = lax.rem(i, 2)

    def gather(tile_idx, dst_slot):
        base = tile_idx * tm

        def body(r, carry):
            for k in range(TOP_K):
                _slab_copy(rows_hbm, pos_ref[k * (n * tm) + base + r], buf.at[dst_slot, k], r,
                           sem.at[dst_slot]).start()
            return carry

        lax.fori_loop(0, tm, body, 0, unroll=4)

    @pl.when(i == 0)
    def _():
        gather(0, 0)

    @pl.when(i + 1 < n)
    def _():
        gather(i + 1, 1 - slot)

    pltpu.make_async_copy(buf.at[slot], buf.at[slot], sem.at[slot]).wait()
    route = route_ref[...]
    g_a = route[:, ROUTE_GATE:ROUTE_GATE + 1]
    g_b = route[:, ROUTE_GATE + 1:ROUTE_GATE + 2]
    f = g_a * _rows_from_slabs(buf.at[slot, 0], tm) + g_b * _rows_from_slabs(buf.at[slot, 1], tm)
    y_ref[...] = _rms(hp_ref[...] + f, gfin_ref[...])


def _combine(pos, rows, hp, route, g_final, tm):
    t, d = hp.shape
    grid_spec = pltpu.PrefetchScalarGridSpec(
        num_scalar_prefetch=1,
        grid=(t // tm,),
        in_specs=[
            pl.BlockSpec(memory_space=pl.ANY),
            pl.BlockSpec((tm, d), lambda i, pos: (i, 0)),
            pl.BlockSpec((tm, LANES), lambda i, pos: (i, 0)),
            pl.BlockSpec((1, d), lambda i, pos: (0, 0)),
        ],
        out_specs=pl.BlockSpec((tm, d), lambda i, pos: (i, 0)),
        scratch_shapes=[pltpu.VMEM((2, TOP_K, tm * SLAB_ROWS, LANES), F32), pltpu.SemaphoreType.DMA((2,))],
    )
    return pl.pallas_call(
        functools.partial(_combine_kernel, tm=tm),
        grid_spec=grid_spec,
        out_shape=jax.ShapeDtypeStruct((t, d), F32),
        compiler_params=_cparams(("arbitrary",)),
        name="moe_combine",
    )(pos, rows, hp, route, g_final)


def _routed_moe(u, hp, route, counts, w_gate, w_up, w_down, g_final, tile, tm_combine):
    t = hp.shape[0]
    n_tiles_max = (TOP_K * t + N_EXPERTS * (tile - 1)) // tile
    pos, tile_expert, row_token, n_tiles = _dispatch_plan(route, counts, tile, n_tiles_max)
    rows = _experts(u, tile_expert, row_token, n_tiles, w_gate, w_up, w_down, tile)
    return _combine(pos, rows, hp, route, g_final, tm_combine)


def _pad_lanes(a, n=LANES):
    return jnp.pad(a, ((0, 0), (0, n - a.shape[1])))


def kernel(x_prompt, x_sample, cache_sb_k, cache_sb_v, cache_fox_k, cache_fox_v, cache_fox_logf, page_table, norm_attn_g, w_in, b_forget, g_sb_out, g_fox_out, w_out, norm_ffn_g, w_router_group, b_router_group, w_router_expert, b_router_expert, w_expert_gate, w_expert_up, w_expert_down, norm_final_g):
    batch, seq, d = x_prompt.shape
    nb = x_sample.shape[0]
    assert w_in.shape[0] == 1, "one trunk layer"
    d_qkv = 3 * (N_SB + N_FOX) * HEAD_DIM

    w_main = w_in[0, :, :d_qkv].astype(BF16)
    w_f = _pad_lanes(w_in[0, :, d_qkv:]).astype(BF16)
    b_f = _pad_lanes(b_forget)
    g_attn = norm_attn_g
    w_out_b = w_out[0].astype(BF16)
    w_r = jnp.concatenate([w_router_expert[0].transpose(1, 0, 2).reshape(d, N_EXPERTS), w_router_group[0]], axis=1)
    w_r = _pad_lanes(w_r)
    w_r_hi = w_r.astype(BF16)
    w_r_lo = (w_r - w_r_hi.astype(F32)).astype(BF16)
    w_r2 = jnp.concatenate([w_r_hi, w_r_lo], axis=1)
    b_r = _pad_lanes(jnp.concatenate([b_router_expert[0].reshape(1, N_EXPERTS), b_router_group], axis=1))
    g_fin = norm_final_g.reshape(1, d)
    experts = (w_expert_gate[0], w_expert_up[0], w_expert_down[0])

    xp = x_prompt.reshape(batch * seq, d)
    z_p, logf_p, *kv_p = _inproj(xp, g_attn, w_main, w_f, b_f, tm=TM_INPROJ)
    logf_p = logf_p[:, :N_FOX].reshape(batch, seq, N_FOX)
    c_rows = _cumsum_rows(logf_p.transpose(0, 2, 1)).reshape(batch * N_FOX, 1, seq)
    o_sb = _sb_attention(z_p, batch, seq, TQ, TK)
    o_fx = _fox_attention(z_p, c_rows, batch, seq, TQ, TK)
    hp_p, u_p, route_p, count_p = _merge(o_sb, o_fx, xp, g_sb_out, g_fox_out, w_out_b, norm_ffn_g, w_r2, b_r,
                                         tm=TM_MERGE)
    y_prompt = _routed_moe(u_p, hp_p, route_p, count_p, *experts, g_fin, TILE_EXPERT, TM_COMBINE)
    y_prompt = y_prompt.reshape(batch, seq, d)

    xs = x_sample.reshape(nb, d)
    z_s, logf_s, *kv_s = _inproj(xs, g_attn, w_main, w_f, b_f, tm=nb)
    logf_s = logf_s[:, :N_FOX]
    per_head = lambda a: a.reshape(nb, N_SB, HEAD_DIM)
    page = cache_fox_logf.shape[2]
    logf_rows = cache_fox_logf[0].reshape(-1, 1, page * N_FOX)
    o_sb_s, o_fx_s = _decode_attention(
        page_table, per_head(z_s[0]), per_head(z_s[3]), per_head(kv_s[2]), per_head(kv_s[3]),
        logf_s.reshape(nb, N_FOX, 1), cache_sb_k[0], cache_sb_v[0], cache_fox_k[0], cache_fox_v[0],
        logf_rows, PAGES_PER_STEP)
    hp_s, u_s, route_s, count_s = _merge(o_sb_s.reshape(nb, -1), o_fx_s.reshape(nb, -1), xs, g_sb_out, g_fox_out,
                                         w_out_b, norm_ffn_g, w_r2, b_r, tm=nb)
    y_sample = _routed_moe(u_s, hp_s, route_s, count_s, *experts, g_fin, nb, nb).reshape(nb, 1, d)

    heads_p = lambda a: a.reshape(1, batch, seq, N_SB, HEAD_DIM)
    heads_s = lambda a: a.reshape(1, nb, 1, N_SB, HEAD_DIM)
    return (y_prompt, y_sample,
            heads_p(kv_p[0]), heads_p(kv_p[1]), heads_p(kv_p[2]), heads_p(kv_p[3]), logf_p[None],
            heads_s(kv_s[0]), heads_s(kv_s[1]), heads_s(kv_s[2]), heads_s(kv_s[3]),
            logf_s.reshape(1, nb, 1, N_FOX))
```

```python
import functools

import jax
import jax.numpy as jnp
from jax import lax
from jax.experimental import pallas as pl
from jax.experimental.pallas import tpu as pltpu

HEAD_DIM = 128
N_SB = 8
N_FOX = 8
N_GROUPS = 4
EXPERTS_PER_GROUP = 8
N_EXPERTS = N_GROUPS * EXPERTS_PER_GROUP
TOP_K = 2
SCALE = HEAD_DIM ** -0.5
LOG2E = 1.4426950408889634
RMS_EPS = 1e-6
LANES = 128
D_MODEL = 2048
QKV_BLOCK = N_SB * HEAD_DIM
SLAB_ROWS = D_MODEL // LANES
HALF_ROWS = SLAB_ROWS // 2
VMEM_LIMIT = 56 * 1024 * 1024
NEG_INF = float("-inf")
BF16 = jnp.bfloat16
F32 = jnp.float32
I32 = jnp.int32

TM_INPROJ = 512
TQ = 256
TK = 256
HEADS_PER_STEP = 8
PAGES_PER_STEP = 8
TM_MERGE = 512
TILE_EXPERT = 256
TM_COMBINE = 256


def _cparams(sem):
    return pltpu.CompilerParams(dimension_semantics=sem, vmem_limit_bytes=VMEM_LIMIT)


def _rms(x, g):
    return x * lax.rsqrt(jnp.mean(x * x, axis=-1, keepdims=True) + RMS_EPS) * g


def _log_sigmoid(x):
    return jnp.minimum(x, 0.0) - jnp.log(1.0 + jnp.exp(-jnp.abs(x)))


def _split_terms(x, terms=2):
    pieces = []
    rest = x
    for _ in range(terms - 1):
        head = rest.astype(BF16).astype(F32)
        pieces.append(head)
        rest = rest - head
    pieces.append(rest)
    return pieces


def _log2_sigmoid_neg(z2):
    return jnp.log(1.0 + jnp.exp2(-jnp.abs(z2))) * (-LOG2E) - jnp.maximum(z2, 0.0)


def _split_hi_lo_lanes(x):
    hi = x.astype(BF16).astype(F32)
    return jnp.concatenate([hi, x - hi], axis=1).astype(BF16)


def _dot_nt(a, b):
    return lax.dot_general(a, b, (((1,), (1,)), ((), ())), preferred_element_type=F32)


def _inproj_kernel(x_ref, g_ref, w_ref, wf_ref, bf_ref, z_ref, logf_ref, ksb_ref, vsb_ref, kfx_ref, vfx_ref,
                   u_ref):
    j = pl.program_id(1)
    tm = x_ref.shape[0]

    @pl.when(j == 0)
    def _():
        u = _rms(x_ref[...], g_ref[...]).astype(BF16)
        u_ref[...] = u
        f = jnp.dot(u, wf_ref[...], preferred_element_type=F32) + bf_ref[...]
        logf_ref[...] = _log_sigmoid(f)

    z_ref[...] = jnp.dot(u_ref[...], w_ref[...], preferred_element_type=F32)

    for block, ref in ((1, ksb_ref), (2, vsb_ref), (4, kfx_ref), (5, vfx_ref)):
        @pl.when(j == block)
        def _(ref=ref):
            for h in range(N_SB):
                ref[pl.ds(h, tm, stride=N_SB), :] = z_ref[:, h * HEAD_DIM:(h + 1) * HEAD_DIM]


def _inproj(x, g, w_main, w_f, b_f, tm):
    t, d = x.shape
    n_blocks = w_main.shape[1] // QKV_BLOCK
    kv_spec = pl.BlockSpec((tm * N_SB, HEAD_DIM), lambda i, j: (i, 0))
    kv_shape = jax.ShapeDtypeStruct((t * N_SB, HEAD_DIM), F32)
    return pl.pallas_call(
        _inproj_kernel,
        grid=(t // tm, n_blocks),
        in_specs=[
            pl.BlockSpec((tm, d), lambda i, j: (i, 0)),
            pl.BlockSpec((1, d), lambda i, j: (0, 0)),
            pl.BlockSpec((d, QKV_BLOCK), lambda i, j: (0, j)),
            pl.BlockSpec((d, LANES), lambda i, j: (0, 0)),
            pl.BlockSpec((1, LANES), lambda i, j: (0, 0)),
        ],
        out_specs=[
            pl.BlockSpec((None, tm, QKV_BLOCK), lambda i, j: (j, i, 0)),
            pl.BlockSpec((tm, LANES), lambda i, j: (i, 0)),
            kv_spec, kv_spec, kv_spec, kv_spec,
        ],
        out_shape=[
            jax.ShapeDtypeStruct((n_blocks, t, QKV_BLOCK), F32),
            jax.ShapeDtypeStruct((t, LANES), F32),
            kv_shape, kv_shape, kv_shape, kv_shape,
        ],
        scratch_shapes=[pltpu.VMEM((tm, d), BF16)],
        compiler_params=_cparams(("arbitrary", "arbitrary")),
        name="inproj",
    )(x, g, w_main, w_f, b_f)


def _cumsum_kernel(x_ref, c_ref, *, chunk):
    s = x_ref.shape[-1]
    r = lax.broadcasted_iota(I32, (chunk, chunk), 0)
    c = lax.broadcasted_iota(I32, (chunk, chunk), 1)
    upper = (r <= c).astype(BF16)
    carry = jnp.zeros((x_ref.shape[0], 1), F32)
    for k in range(s // chunk):
        x = x_ref[:, k * chunk:(k + 1) * chunk]
        stacked = jnp.concatenate(_split_terms(x, 3), axis=0).astype(BF16)
        parts = jnp.dot(stacked, upper, preferred_element_type=F32)
        n = x.shape[0]
        cs = parts[:n] + parts[n:2 * n] + parts[2 * n:] + carry
        c_ref[:, k * chunk:(k + 1) * chunk] = cs
        carry = cs[:, chunk - 1:chunk]


def _cumsum_rows(x):
    b, h, s = x.shape
    return pl.pallas_call(
        functools.partial(_cumsum_kernel, chunk=256),
        grid=(b,),
        in_specs=[pl.BlockSpec((None, h, s), lambda i: (i, 0, 0))],
        out_specs=pl.BlockSpec((None, h, s), lambda i: (i, 0, 0)),
        out_shape=jax.ShapeDtypeStruct((b, h, s), F32),
        compiler_params=_cparams(("arbitrary",)),
        name="logf_cumsum",
    )(x)


def _head_rows(ref, start, n, head, heads):
    del heads
    return ref[pl.ds(start, n), head * HEAD_DIM:(head + 1) * HEAD_DIM].astype(BF16)


def _sb_kernel(q_ref, k_ref, v_ref, o_ref, *, tq, tk, heads):
    i = pl.program_id(1)
    r = lax.broadcasted_iota(I32, (2 * tk, tk), 0)
    c = lax.broadcasted_iota(I32, (2 * tk, tk), 1)
    not_before2 = ((r & (tk - 1)) >= c).astype(BF16)
    strictly_causal = lax.broadcasted_iota(I32, (tq, tk), 1) < lax.broadcasted_iota(I32, (tq, tk), 0)
    cols = [slice(g * HEAD_DIM, (g + 1) * HEAD_DIM) for g in range(heads)]
    q = [(q_ref[:, cs] * (SCALE * LOG2E)).astype(BF16) for cs in cols]
    hs = range(heads)

    def block(j, carry, diag):
        start = pl.multiple_of(j * tk, tk)
        z = [_dot_nt(q[g], _head_rows(k_ref, start, tk, g, heads)) for g in hs]
        log_stay = [_log2_sigmoid_neg(z[g]) for g in hs]
        if diag:
            log_stay = [jnp.where(strictly_causal, x, 0.0) for x in log_stay]
        incl = [jnp.dot(_split_hi_lo_lanes(log_stay[g]), not_before2, preferred_element_type=F32) for g in hs]
        a = [jnp.exp2(z[g] + incl[g] + carry[g][0]) for g in hs]
        if diag:
            a = [jnp.where(strictly_causal, x, 0.0) for x in a]
        pv = [jnp.dot(a[g].astype(BF16), _head_rows(v_ref, start, tk, g, heads),
                      preferred_element_type=F32) for g in hs]
        return tuple((carry[g][0] + incl[g][:, 0:1], carry[g][1] + pv[g]) for g in hs)

    init = tuple((jnp.zeros((tq, 1), F32), jnp.zeros((tq, HEAD_DIM), F32)) for _ in hs)
    carry = block(i, init, True)
    carry = lax.fori_loop(0, i, lambda t, cy: block(i - 1 - t, cy, False), carry)
    for g in range(heads):
        o_ref[:, cols[g]] = carry[g][1]


def _sb_attention(z, batch, seq, tq, tk):
    nq = seq // tq
    kv_spec = lambda which: pl.BlockSpec((None, seq, QKV_BLOCK), lambda b, i: (which, b, 0))
    return pl.pallas_call(
        functools.partial(_sb_kernel, tq=tq, tk=tk, heads=N_SB),
        grid=(batch, nq),
        in_specs=[pl.BlockSpec((None, tq, QKV_BLOCK), lambda b, i: (0, b * nq + i, 0)), kv_spec(1), kv_spec(2)],
        out_specs=pl.BlockSpec((tq, QKV_BLOCK), lambda b, i: (b * nq + i, 0)),
        out_shape=jax.ShapeDtypeStruct((batch * seq, QKV_BLOCK), F32),
        compiler_params=_cparams(("arbitrary", "arbitrary")),
        name="sb_attention",
    )(z, z, z)


def _fox_kernel(q_ref, k_ref, v_ref, c_ref, o_ref, *, tq, tk, heads):
    i = pl.program_id(1)
    causal = lax.broadcasted_iota(I32, (tq, tk), 1) <= lax.broadcasted_iota(I32, (tq, tk), 0)
    cols = [slice(g * HEAD_DIM, (g + 1) * HEAD_DIM) for g in range(heads)]
    q = [(q_ref[:, cs] * (SCALE * LOG2E)).astype(BF16) for cs in cols]
    q_start = pl.multiple_of(i * tq, tq)
    c_q0 = [c_ref[g, :, pl.ds(q_start, tq)][:, 0:1] for g in range(heads)]
    hs = range(heads)

    def block(j, carry, diag):
        start = pl.multiple_of(j * tk, tk)
        s = [_dot_nt(q[g], _head_rows(k_ref, start, tk, g, heads))
             + (c_q0[g] - c_ref[g, :, pl.ds(start, tk)]) * LOG2E for g in hs]
        if diag:
            s = [jnp.where(causal, x, NEG_INF) for x in s]
        m_new = [jnp.maximum(carry[g][0], jnp.max(s[g], axis=-1, keepdims=True)) for g in hs]
        p = [jnp.exp2(s[g] - m_new[g]) for g in hs]
        alpha = [jnp.exp2(carry[g][0] - m_new[g]) for g in hs]
        pv = [jnp.dot(p[g].astype(BF16), _head_rows(v_ref, start, tk, g, heads),
                      preferred_element_type=F32) for g in hs]
        return tuple((m_new[g], alpha[g] * carry[g][1] + jnp.sum(p[g], axis=-1, keepdims=True),
                      alpha[g] * carry[g][2] + pv[g]) for g in hs)

    init = tuple((jnp.full((tq, 1), NEG_INF, F32), jnp.zeros((tq, 1), F32), jnp.zeros((tq, HEAD_DIM), F32))
                 for _ in hs)
    carry = block(i, init, True)
    carry = lax.fori_loop(0, i, lambda t, cy: block(i - 1 - t, cy, False), carry)
    for g in range(heads):
        o_ref[:, cols[g]] = carry[g][2] / carry[g][1]


def _fox_attention(z, c_rows, batch, seq, tq, tk):
    nq = seq // tq
    kv_spec = lambda which: pl.BlockSpec((None, seq, QKV_BLOCK), lambda b, i: (which, b, 0))
    return pl.pallas_call(
        functools.partial(_fox_kernel, tq=tq, tk=tk, heads=N_FOX),
        grid=(batch, nq),
        in_specs=[pl.BlockSpec((None, tq, QKV_BLOCK), lambda b, i: (3, b * nq + i, 0)), kv_spec(4), kv_spec(5),
                  pl.BlockSpec((N_FOX, 1, seq), lambda b, i: (b, 0, 0))],
        out_specs=pl.BlockSpec((tq, QKV_BLOCK), lambda b, i: (b * nq + i, 0)),
        out_shape=jax.ShapeDtypeStruct((batch * seq, QKV_BLOCK), F32),
        compiler_params=_cparams(("arbitrary", "arbitrary")),
        name="fox_attention",
    )(z, z, z, c_rows)


def _reverse_cumsum_lanes(x, lane):
    n = x.shape[-1]
    d = N_SB
    while d < n:
        shifted = pltpu.roll(x, n - d, 1)
        x = x + jnp.where(lane + d < n, shifted, 0.0)
        d *= 2
    return x


def _decode_kernel(pt_ref, qs_ref, qf_ref, kown_ref, vown_ref, lfown_ref, *refs, pages_per_step):
    del pt_ref
    pp = pages_per_step
    ksb = refs[0:pp]
    vsb = refs[pp:2 * pp]
    kfx = refs[2 * pp:3 * pp]
    vfx = refs[3 * pp:4 * pp]
    lfc = refs[4 * pp:5 * pp]
    osb_ref, ofx_ref = refs[5 * pp:5 * pp + 2]
    acc_sb, acc_fx, run_sb, run_fx, m_ref, l_ref = refs[5 * pp + 2:]
    j = pl.program_id(1)
    page_rows = ksb[0].shape[0] * ksb[0].shape[1]
    rows = pp * page_rows

    qs = qs_ref[...] * SCALE
    qf = qf_ref[...] * SCALE

    @pl.when(j == 0)
    def _():
        acc_sb[...] = jnp.zeros_like(acc_sb)
        run_sb[...] = jnp.zeros_like(run_sb)
        m_ref[...] = jnp.sum(qf * kown_ref[...], axis=-1, keepdims=True)
        l_ref[...] = jnp.ones_like(l_ref)
        acc_fx[...] = vown_ref[...]
        run_fx[...] = lfown_ref[...]

    def rows_of(page_refs):
        return jnp.concatenate([page_refs[p][...].reshape(page_rows, HEAD_DIM).astype(BF16)
                                for p in reversed(range(pp))], axis=0)

    lane = lax.broadcasted_iota(I32, (N_SB, rows), 1)
    head = lax.broadcasted_iota(I32, (N_SB, rows), 0)
    valid = (lane & (N_SB - 1)) == head

    z = _dot_nt(qs.astype(BF16), rows_of(ksb))
    log_stay = jnp.where(valid, _log_sigmoid(-z), 0.0)
    incl = _reverse_cumsum_lanes(log_stay, lane)
    run = run_sb[...]
    a = jnp.where(valid, jnp.exp(z + incl + run), 0.0)
    acc_sb[...] += jnp.dot(a.astype(BF16), rows_of(vsb), preferred_element_type=F32)
    run_sb[...] = run + jnp.sum(log_stay, axis=-1, keepdims=True)

    lf = jnp.where(valid, jnp.concatenate([lfc[p][...] for p in reversed(range(pp))], axis=1), 0.0)
    incl = _reverse_cumsum_lanes(lf, lane)
    run = run_fx[...]
    s = jnp.where(valid, _dot_nt(qf.astype(BF16), rows_of(kfx)) + (incl - lf) + run, NEG_INF)
    m_old = m_ref[...]
    m_new = jnp.maximum(m_old, jnp.max(s, axis=-1, keepdims=True))
    alpha = jnp.exp(m_old - m_new)
    pr = jnp.exp(s - m_new)
    l_ref[...] = alpha * l_ref[...] + jnp.sum(pr, axis=-1, keepdims=True)
    acc_fx[...] = alpha * acc_fx[...] + jnp.dot(pr.astype(BF16), rows_of(vfx), preferred_element_type=F32)
    m_ref[...] = m_new
    run_fx[...] = run + jnp.sum(lf, axis=-1, keepdims=True)

    @pl.when(j == pl.num_programs(1) - 1)
    def _():
        osb_ref[...] = acc_sb[...]
        ofx_ref[...] = acc_fx[...] / l_ref[...]


def _decode_attention(page_table, q_sb, q_fx, k_own, v_own, lf_own, c_sb_k, c_sb_v, c_fx_k, c_fx_v,
                      c_logf_rows, pages_per_step):
    nb, n_pages = page_table.shape
    page, heads, hd = c_sb_k.shape[1:]
    rows = page * heads
    pp = pages_per_step
    steps = n_pages // pp

    def page_of(p):
        return lambda b, j, pt: (pt[b * n_pages + (n_pages - 1 - (j * pp + p))], 0, 0, 0)

    def lf_page_of(p):
        return lambda b, j, pt: (pt[b * n_pages + (n_pages - 1 - (j * pp + p))], 0, 0)

    per_b = pl.BlockSpec((None, heads, hd), lambda b, j, pt: (b, 0, 0))
    cache_specs = [pl.BlockSpec((None, page, heads, hd), page_of(p)) for p in range(pp)]
    lf_specs = [pl.BlockSpec((None, 1, rows), lf_page_of(p)) for p in range(pp)]
    grid_spec = pltpu.PrefetchScalarGridSpec(
        num_scalar_prefetch=1,
        grid=(nb, steps),
        in_specs=[per_b, per_b, per_b, per_b, pl.BlockSpec((None, heads, 1), lambda b, j, pt: (b, 0, 0))]
        + cache_specs * 4 + lf_specs,
        out_specs=[per_b, per_b],
        scratch_shapes=[pltpu.VMEM((heads, hd), F32), pltpu.VMEM((heads, hd), F32),
                        pltpu.VMEM((heads, 1), F32), pltpu.VMEM((heads, 1), F32),
                        pltpu.VMEM((heads, 1), F32), pltpu.VMEM((heads, 1), F32)],
    )
    return pl.pallas_call(
        functools.partial(_decode_kernel, pages_per_step=pp),
        grid_spec=grid_spec,
        out_shape=[jax.ShapeDtypeStruct((nb, heads, hd), F32)] * 2,
        compiler_params=_cparams(("arbitrary", "arbitrary")),
        name="decode_attention",
    )(page_table.reshape(-1), q_sb, q_fx, k_own, v_own, lf_own,
      *([c_sb_k] * pp), *([c_sb_v] * pp), *([c_fx_k] * pp), *([c_fx_v] * pp), *([c_logf_rows] * pp))


ROUTE_EXPERT, ROUTE_GATE, ROUTE_RANK = 0, 2, 4


def _merge_kernel(osb_ref, ofx_ref, x_ref, gsb_ref, gfx_ref, wout_ref, gffn_ref, wr_ref, br_ref,
                  hp_ref, u_ref, route_ref, count_ref, base_ref):
    @pl.when(pl.program_id(0) == 0)
    def _():
        base_ref[...] = jnp.zeros_like(base_ref)

    o = jnp.concatenate([_rms(osb_ref[...], gsb_ref[...]), _rms(ofx_ref[...], gfx_ref[...])], axis=-1)
    y = jnp.dot(o.astype(BF16), wout_ref[...], preferred_element_type=F32)
    hp = x_ref[...] + y
    hp_ref[...] = hp
    u = _rms(hp, gffn_ref[...])
    tm = u.shape[0]
    for s in range(SLAB_ROWS):
        u_ref[s // HALF_ROWS, pl.ds(s % HALF_ROWS, tm, stride=HALF_ROWS), :] = u[:, s * LANES:(s + 1) * LANES]

    stacked = jnp.concatenate(_split_terms(u), axis=0).astype(BF16)
    parts = jnp.dot(stacked, wr_ref[...], preferred_element_type=F32)
    both = parts[:tm] + parts[tm:]
    logits = both[:, :LANES] + both[:, LANES:] + br_ref[...]

    lane = lax.broadcasted_iota(I32, (tm, LANES), 1).astype(F32)
    is_group = (lane >= N_EXPERTS) & (lane < N_EXPERTS + N_GROUPS)
    big = float(1 << 20)
    lg = jnp.where(is_group, logits, NEG_INF)
    lg_max = jnp.max(lg, axis=-1, keepdims=True)
    gsel = jnp.min(jnp.where(lg == lg_max, lane, big), axis=-1, keepdims=True) - N_EXPERTS
    g1 = 1.0 / jnp.sum(jnp.exp(lg - lg_max), axis=-1, keepdims=True)

    in_group = (lane >= gsel * EXPERTS_PER_GROUP) & (lane < (gsel + 1) * EXPERTS_PER_GROUP)
    le = jnp.where(in_group, logits, NEG_INF)
    v_a = jnp.max(le, axis=-1, keepdims=True)
    i_a = jnp.min(jnp.where(le == v_a, lane, big), axis=-1, keepdims=True)
    le_b = jnp.where(lane == i_a, NEG_INF, le)
    v_b = jnp.max(le_b, axis=-1, keepdims=True)
    i_b = jnp.min(jnp.where(le_b == v_b, lane, big), axis=-1, keepdims=True)
    e_b = jnp.exp(v_b - v_a)
    gate_a = g1 / (1.0 + e_b)
    gate_b = g1 * e_b / (1.0 + e_b)

    onehot = jnp.where((lane == i_a) | (lane == i_b), 1.0, 0.0)
    rr = lax.broadcasted_iota(I32, (tm, tm), 0)
    cc = lax.broadcasted_iota(I32, (tm, tm), 1)
    count_dtype = BF16 if tm % 16 == 0 else F32
    earlier = (cc < rr).astype(count_dtype)
    rank_all = jnp.dot(earlier, onehot.astype(count_dtype), preferred_element_type=F32) + base_ref[...]
    rank_a = jnp.sum(jnp.where(lane == i_a, rank_all, 0.0), axis=-1, keepdims=True)
    rank_b = jnp.sum(jnp.where(lane == i_b, rank_all, 0.0), axis=-1, keepdims=True)
    total = base_ref[...] + jnp.sum(onehot, axis=0, keepdims=True)
    base_ref[...] = total
    count_ref[...] = total

    route = jnp.zeros((tm, LANES), F32)
    for k, val in ((ROUTE_EXPERT, i_a), (ROUTE_EXPERT + 1, i_b), (ROUTE_GATE, gate_a), (ROUTE_GATE + 1, gate_b),
                   (ROUTE_RANK, rank_a), (ROUTE_RANK + 1, rank_b)):
        route = jnp.where(lane == k, val, route)
    route_ref[...] = route


def _merge(o_sb, o_fx, x, g_sb, g_fx, w_out, g_ffn, w_r, b_r, tm):
    t, d = x.shape
    row = lambda n: pl.BlockSpec((tm, n), lambda i: (i, 0))
    const = lambda a: pl.BlockSpec(a.shape, lambda i: (0,) * a.ndim)
    return pl.pallas_call(
        _merge_kernel,
        grid=(t // tm,),
        in_specs=[row(o_sb.shape[1]), row(o_fx.shape[1]), row(d), const(g_sb), const(g_fx), const(w_out),
                  const(g_ffn), const(w_r), const(b_r)],
        out_specs=[row(d), pl.BlockSpec((2, tm * HALF_ROWS, LANES), lambda i: (0, i, 0)), row(LANES),
                   pl.BlockSpec((1, LANES), lambda i: (0, 0))],
        out_shape=[jax.ShapeDtypeStruct((t, d), F32), jax.ShapeDtypeStruct((2, t * HALF_ROWS, LANES), F32),
                   jax.ShapeDtypeStruct((t, LANES), F32), jax.ShapeDtypeStruct((1, LANES), F32)],
        scratch_shapes=[pltpu.VMEM((1, LANES), F32)],
        compiler_params=_cparams(("arbitrary",)),
        name="merge_router",
    )(o_sb, o_fx, x, g_sb, g_fx, w_out, g_ffn, w_r, b_r)


def _dispatch_plan(route, counts, tile, n_tiles_max):
    t = route.shape[0]
    expert = route[:, ROUTE_EXPERT:ROUTE_EXPERT + TOP_K].T.astype(I32)
    rank = route[:, ROUTE_RANK:ROUTE_RANK + TOP_K].T.astype(I32)
    count = counts[0, :N_EXPERTS].astype(I32)
    tiles = (count + tile - 1) // tile
    tile_end = jnp.cumsum(tiles)
    first_row = ((tile_end - tiles) * tile).astype(F32)
    row0 = jnp.dot(jax.nn.one_hot(expert, N_EXPERTS, dtype=F32), first_row, precision=lax.Precision.HIGHEST)
    pos = (row0.astype(I32) + rank).reshape(-1)
    tile_ids = jnp.arange(n_tiles_max, dtype=I32)
    tile_expert = jnp.minimum(jnp.sum((tile_end[None, :] <= tile_ids[:, None]).astype(I32), axis=1), N_EXPERTS - 1)
    row_token = _row_tokens(pos, t, n_tiles_max * tile)
    return pos, tile_expert, row_token, tile_end[-1:].astype(I32)


def _row_token_kernel(pos_ref, rt_ref, *, t):
    def clear(r, carry):
        rt_ref[r] = 0
        return carry

    lax.fori_loop(0, rt_ref.shape[0], clear, 0, unroll=8)

    def place(a, carry):
        token = a
        for _ in range(TOP_K - 1):
            token = jnp.where(token >= t, token - t, token)
        rt_ref[pos_ref[a]] = token
        return carry

    lax.fori_loop(0, pos_ref.shape[0], place, 0, unroll=8)


def _row_tokens(pos, t, n_rows):
    return pl.pallas_call(
        functools.partial(_row_token_kernel, t=t),
        grid_spec=pltpu.PrefetchScalarGridSpec(
            num_scalar_prefetch=1, grid=(1,), in_specs=[],
            out_specs=pl.BlockSpec(memory_space=pltpu.SMEM)),
        out_shape=jax.ShapeDtypeStruct((n_rows,), I32),
        compiler_params=_cparams(("arbitrary",)),
        name="moe_row_tokens",
    )(pos)


def _slab_copy(src_hbm, src_row, dst, dst_row, sem):
    return pltpu.make_async_copy(src_hbm.at[pl.ds(pl.multiple_of(src_row * SLAB_ROWS, SLAB_ROWS), SLAB_ROWS), :],
                                 dst.at[pl.ds(pl.multiple_of(dst_row * SLAB_ROWS, SLAB_ROWS), SLAB_ROWS), :], sem)


def _rows_from_slabs(slabs, n):
    return jnp.concatenate([slabs[pl.ds(s, n, stride=SLAB_ROWS), :] for s in range(SLAB_ROWS)], axis=1)


def _expert_kernel(te_ref, rt_ref, nt_ref, u_hbm, wg_ref, wu_ref, *refs, tile, half):
    final = half == 1
    if final:
        wd_ref, part_ref, out_ref, u_vmem, xs, sem, wg_b, wu_b, wd_b = refs
    else:
        out_ref, u_vmem, xs, sem, wg_b, wu_b = refs
    i = pl.program_id(0)
    nt = nt_ref[0]
    d_e = wg_b.shape[1]

    @pl.when(i == 0)
    def _():
        load = pltpu.make_async_copy(u_hbm.at[half], u_vmem, sem.at[0])
        load.start()
        load.wait()

    @pl.when(i < nt)
    def _():
        @pl.when((i == 0) | (te_ref[i] != te_ref[jnp.maximum(i - 1, 0)]))
        def _():
            wg_b[...] = wg_ref[...].astype(BF16)
            wu_b[...] = wu_ref[...].astype(BF16)
            if final:
                wd_b[...] = wd_ref[...].astype(BF16)

        base = i * tile

        def body(r, carry):
            src = pl.multiple_of(rt_ref[base + r] * HALF_ROWS, HALF_ROWS)
            xs[pl.ds(pl.multiple_of(r * HALF_ROWS, HALF_ROWS), HALF_ROWS), :] = u_vmem[pl.ds(src, HALF_ROWS), :]
            return carry

        lax.fori_loop(0, tile, body, 0, unroll=8)
        x = jnp.concatenate([xs[pl.ds(s, tile, stride=HALF_ROWS), :] for s in range(HALF_ROWS)],
                            axis=1).astype(BF16)
        gate = jnp.dot(x, wg_b[...], preferred_element_type=F32)
        up = jnp.dot(x, wu_b[...], preferred_element_type=F32)
        if final:
            gate = gate + part_ref[:, :d_e]
            up = up + part_ref[:, d_e:]
            h = gate * (1.0 / (1.0 + jnp.exp(-gate))) * up
            y = jnp.dot(h.astype(BF16), wd_b[...], preferred_element_type=F32)
            for s in range(SLAB_ROWS):
                out_ref[pl.ds(s, tile, stride=SLAB_ROWS), :] = y[:, s * LANES:(s + 1) * LANES]
        else:
            out_ref[:, :d_e] = gate
            out_ref[:, d_e:] = up

    @pl.when(i >= nt)
    def _():
        out_ref[...] = jnp.zeros_like(out_ref)


def _experts(u_halves, tile_expert, row_token, n_tiles, w_gate, w_up, w_down, tile):
    n_e, d, d_e = w_gate.shape
    half_d = HALF_ROWS * LANES
    n_tiles_max = tile_expert.shape[0]
    last = lambda i, nt: jnp.minimum(i, nt[0] - 1)

    def call(half, extra_in, extra_specs, out_spec, out_shape, extra_scratch):
        grid_spec = pltpu.PrefetchScalarGridSpec(
            num_scalar_prefetch=3,
            grid=(n_tiles_max,),
            in_specs=[
                pl.BlockSpec(memory_space=pl.ANY),
                pl.BlockSpec((None, half_d, d_e), lambda i, te, rt, nt: (te[last(i, nt)], half, 0)),
                pl.BlockSpec((None, half_d, d_e), lambda i, te, rt, nt: (te[last(i, nt)], half, 0)),
            ] + extra_specs,
            out_specs=out_spec,
            scratch_shapes=[pltpu.VMEM(u_halves.shape[1:], F32), pltpu.VMEM((tile * HALF_ROWS, LANES), F32),
                            pltpu.SemaphoreType.DMA((1,)),
                            pltpu.VMEM((half_d, d_e), BF16), pltpu.VMEM((half_d, d_e), BF16)] + extra_scratch,
        )
        return pl.pallas_call(
            functools.partial(_expert_kernel, tile=tile, half=half),
            grid_spec=grid_spec,
            out_shape=out_shape,
            compiler_params=_cparams(("arbitrary",)),
            name=f"moe_experts_half{half}",
        )(tile_expert, row_token, n_tiles, u_halves, w_gate, w_up, *extra_in)

    part_spec = pl.BlockSpec((tile, 2 * d_e), lambda i, te, rt, nt: (i, 0))
    part = call(0, [], [], part_spec, jax.ShapeDtypeStruct((n_tiles_max * tile, 2 * d_e), F32), [])
    return call(1, [w_down, part],
                [pl.BlockSpec((None, d_e, d), lambda i, te, rt, nt: (te[last(i, nt)], 0, 0)), part_spec],
                pl.BlockSpec((tile * SLAB_ROWS, LANES), lambda i, te, rt, nt: (i, 0)),
                jax.ShapeDtypeStruct((n_tiles_max * tile * SLAB_ROWS, LANES), F32),
                [pltpu.VMEM((d_e, d), BF16)])


def _combine_kernel(pos_ref, rows_hbm, hp_ref, route_ref, gfin_ref, y_ref, buf, sem, *, tm):
    i = pl.program_id(0)
    n = pl.num_programs(0)
    slot = lax.rem(i, 2)

    def gather(tile_idx, dst_slot):
        base = tile_idx * tm

        def body(r, carry):
            for k in range(TOP_K):
                _slab_copy(rows_hbm, pos_ref[k * (n * tm) + base + r], buf.at[dst_slot, k], r,
                           sem.at[dst_slot]).start()
            return carry

        lax.fori_loop(0, tm, body, 0, unroll=4)

    @pl.when(i == 0)
    def _():
        gather(0, 0)

    @pl.when(i + 1 < n)
    def _():
        gather(i + 1, 1 - slot)

    pltpu.make_async_copy(buf.at[slot], buf.at[slot], sem.at[slot]).wait()
    route = route_ref[...]
    g_a = route[:, ROUTE_GATE:ROUTE_GATE + 1]
    g_b = route[:, ROUTE_GATE + 1:ROUTE_GATE + 2]
    f = g_a * _rows_from_slabs(buf.at[slot, 0], tm) + g_b * _rows_from_slabs(buf.at[slot, 1], tm)
    y_ref[...] = _rms(hp_ref[...] + f, gfin_ref[...])


def _combine(pos, rows, hp, route, g_final, tm):
    t, d = hp.shape
    grid_spec = pltpu.PrefetchScalarGridSpec(
        num_scalar_prefetch=1,
        grid=(t // tm,),
        in_specs=[
            pl.BlockSpec(memory_space=pl.ANY),
            pl.BlockSpec((tm, d), lambda i, pos: (i, 0)),
            pl.BlockSpec((tm, LANES), lambda i, pos: (i, 0)),
            pl.BlockSpec((1, d), lambda i, pos: (0, 0)),
        ],
        out_specs=pl.BlockSpec((tm, d), lambda i, pos: (i, 0)),
        scratch_shapes=[pltpu.VMEM((2, TOP_K, tm * SLAB_ROWS, LANES), F32), pltpu.SemaphoreType.DMA((2,))],
    )
    return pl.pallas_call(
        functools.partial(_combine_kernel, tm=tm),
        grid_spec=grid_spec,
        out_shape=jax.ShapeDtypeStruct((t, d), F32),
        compiler_params=_cparams(("arbitrary",)),
        name="moe_combine",
    )(pos, rows, hp, route, g_final)


def _routed_moe(u, hp, route, counts, w_gate, w_up, w_down, g_final, tile, tm_combine):
    t = hp.shape[0]
    n_tiles_max = (TOP_K * t + N_EXPERTS * (tile - 1)) // tile
    pos, tile_expert, row_token, n_tiles = _dispatch_plan(route, counts, tile, n_tiles_max)
    rows = _experts(u, tile_expert, row_token, n_tiles, w_gate, w_up, w_down, tile)
    return _combine(pos, rows, hp, route, g_final, tm_combine)


def _pad_lanes(a, n=LANES):
    return jnp.pad(a, ((0, 0), (0, n - a.shape[1])))


def kernel(x_prompt, x_sample, cache_sb_k, cache_sb_v, cache_fox_k, cache_fox_v, cache_fox_logf, page_table, norm_attn_g, w_in, b_forget, g_sb_out, g_fox_out, w_out, norm_ffn_g, w_router_group, b_router_group, w_router_expert, b_router_expert, w_expert_gate, w_expert_up, w_expert_down, norm_final_g):
    batch, seq, d = x_prompt.shape
    nb = x_sample.shape[0]
    assert w_in.shape[0] == 1, "one trunk layer"
    d_qkv = 3 * (N_SB + N_FOX) * HEAD_DIM

    w_main = w_in[0, :, :d_qkv].astype(BF16)
    w_f = _pad_lanes(w_in[0, :, d_qkv:]).astype(BF16)
    b_f = _pad_lanes(b_forget)
    g_attn = norm_attn_g
    w_out_b = w_out[0].astype(BF16)
    w_r = jnp.concatenate([w_router_expert[0].transpose(1, 0, 2).reshape(d, N_EXPERTS), w_router_group[0]], axis=1)
    w_r = _pad_lanes(w_r)
    w_r_hi = w_r.astype(BF16)
    w_r_lo = (w_r - w_r_hi.astype(F32)).astype(BF16)
    w_r2 = jnp.concatenate([w_r_hi, w_r_lo], axis=1)
    b_r = _pad_lanes(jnp.concatenate([b_router_expert[0].reshape(1, N_EXPERTS), b_router_group], axis=1))
    g_fin = norm_final_g.reshape(1, d)
    experts = (w_expert_gate[0], w_expert_up[0], w_expert_down[0])

    xp = x_prompt.reshape(batch * seq, d)
    z_p, logf_p, *kv_p = _inproj(xp, g_attn, w_main, w_f, b_f, tm=TM_INPROJ)
    logf_p = logf_p[:, :N_FOX].reshape(batch, seq, N_FOX)
    c_rows = _cumsum_rows(logf_p.transpose(0, 2, 1)).reshape(batch * N_FOX, 1, seq)
    o_sb = _sb_attention(z_p, batch, seq, TQ, TK)
    o_fx = _fox_attention(z_p, c_rows, batch, seq, TQ, TK)
    hp_p, u_p, route_p, count_p = _merge(o_sb, o_fx, xp, g_sb_out, g_fox_out, w_out_b, norm_ffn_g, w_r2, b_r,
                                         tm=TM_MERGE)
    y_prompt = _routed_moe(u_p, hp_p, route_p, count_p, *experts, g_fin, TILE_EXPERT, TM_COMBINE)
    y_prompt = y_prompt.reshape(batch, seq, d)

    xs = x_sample.reshape(nb, d)
    z_s, logf_s, *kv_s = _inproj(xs, g_attn, w_main, w_f, b_f, tm=nb)
    logf_s = logf_s[:, :N_FOX]
    per_head = lambda a: a.reshape(nb, N_SB, HEAD_DIM)
    page = cache_fox_logf.shape[2]
    logf_rows = cache_fox_logf[0].reshape(-1, 1, page * N_FOX)
    o_sb_s, o_fx_s = _decode_attention(
        page_table, per_head(z_s[0]), per_head(z_s[3]), per_head(kv_s[2]), per_head(kv_s[3]),
        logf_s.reshape(nb, N_FOX, 1), cache_sb_k[0], cache_sb_v[0], cache_fox_k[0], cache_fox_v[0],
        logf_rows, PAGES_PER_STEP)
    hp_s, u_s, route_s, count_s = _merge(o_sb_s.reshape(nb, -1), o_fx_s.reshape(nb, -1), xs, g_sb_out, g_fox_out,
                                         w_out_b, norm_ffn_g, w_r2, b_r, tm=nb)
    y_sample = _routed_moe(u_s, hp_s, route_s, count_s, *experts, g_fin, nb, nb).reshape(nb, 1, d)

    heads_p = lambda a: a.reshape(1, batch, seq, N_SB, HEAD_DIM)
    heads_s = lambda a: a.reshape(1, nb, 1, N_SB, HEAD_DIM)
    return (y_prompt, y_sample,
            heads_p(kv_p[0]), heads_p(kv_p[1]), heads_p(kv_p[2]), heads_p(kv_p[3]), logf_p[None],
            heads_s(kv_s[0]), heads_s(kv_s[1]), heads_s(kv_s[2]), heads_s(kv_s[3]),
            logf_s.reshape(1, nb, 1, N_FOX))
```

```python
import functools

import jax
import jax.numpy as jnp
from jax import lax
from jax.experimental import pallas as pl
from jax.experimental.pallas import tpu as pltpu

HEAD_DIM = 128
N_SB = 8
N_FOX = 8
N_GROUPS = 4
EXPERTS_PER_GROUP = 8
N_EXPERTS = N_GROUPS * EXPERTS_PER_GROUP
TOP_K = 2
SCALE = HEAD_DIM ** -0.5
LOG2E = 1.4426950408889634
RMS_EPS = 1e-6
LANES = 128
D_MODEL = 2048
QKV_BLOCK = N_SB * HEAD_DIM
SLAB_ROWS = D_MODEL // LANES
HALF_ROWS = SLAB_ROWS // 2
VMEM_LIMIT = 56 * 1024 * 1024
NEG_INF = float("-inf")
BF16 = jnp.bfloat16
F32 = jnp.float32
I32 = jnp.int32

TM_INPROJ = 512
TQ = 256
TK = 256
SB_HEAD_GROUP = 4
FOX_HEAD_GROUP = 4
PAGES_PER_STEP = 8
TM_MERGE = 512
TILE_EXPERT = 256
TM_COMBINE = 256


def _cparams(sem):
    return pltpu.CompilerParams(dimension_semantics=sem, vmem_limit_bytes=VMEM_LIMIT)


def _rms(x, g):
    return x * lax.rsqrt(jnp.mean(x * x, axis=-1, keepdims=True) + RMS_EPS) * g


def _log_sigmoid(x):
    return jnp.minimum(x, 0.0) - jnp.log(1.0 + jnp.exp(-jnp.abs(x)))


def _split_terms(x, terms=2):
    pieces = []
    rest = x
    for _ in range(terms - 1):
        head = rest.astype(BF16).astype(F32)
        pieces.append(head)
        rest = rest - head
    pieces.append(rest)
    return pieces


def _log2_sigmoid_neg(z2):
    return jnp.log(1.0 + jnp.exp2(-jnp.abs(z2))) * (-LOG2E) - jnp.maximum(z2, 0.0)


def _split_hi_lo_lanes(x):
    hi = x.astype(BF16).astype(F32)
    return jnp.concatenate([hi, x - hi], axis=1).astype(BF16)


def _dot_nt(a, b):
    return lax.dot_general(a, b, (((1,), (1,)), ((), ())), preferred_element_type=F32)


def _inproj_kernel(x_ref, g_ref, w_ref, wf_ref, bf_ref, z_ref, logf_ref, ksb_ref, vsb_ref, kfx_ref, vfx_ref,
                   u_ref):
    j = pl.program_id(1)
    tm = x_ref.shape[0]

    @pl.when(j == 0)
    def _():
        u = _rms(x_ref[...], g_ref[...]).astype(BF16)
        u_ref[...] = u
        f = jnp.dot(u, wf_ref[...], preferred_element_type=F32) + bf_ref[...]
        logf_ref[...] = _log_sigmoid(f)

    z_ref[...] = jnp.dot(u_ref[...], w_ref[...], preferred_element_type=F32)

    for block, ref in ((1, ksb_ref), (2, vsb_ref), (4, kfx_ref), (5, vfx_ref)):
        @pl.when(j == block)
        def _(ref=ref):
            for h in range(N_SB):
                ref[pl.ds(h, tm, stride=N_SB), :] = z_ref[:, h * HEAD_DIM:(h + 1) * HEAD_DIM]


def _inproj(x, g, w_main, w_f, b_f, tm):
    t, d = x.shape
    n_blocks = w_main.shape[1] // QKV_BLOCK
    kv_spec = pl.BlockSpec((tm * N_SB, HEAD_DIM), lambda i, j: (i, 0))
    kv_shape = jax.ShapeDtypeStruct((t * N_SB, HEAD_DIM), F32)
    return pl.pallas_call(
        _inproj_kernel,
        grid=(t // tm, n_blocks),
        in_specs=[
            pl.BlockSpec((tm, d), lambda i, j: (i, 0)),
            pl.BlockSpec((1, d), lambda i, j: (0, 0)),
            pl.BlockSpec((d, QKV_BLOCK), lambda i, j: (0, j)),
            pl.BlockSpec((d, LANES), lambda i, j: (0, 0)),
            pl.BlockSpec((1, LANES), lambda i, j: (0, 0)),
        ],
        out_specs=[
            pl.BlockSpec((None, tm, QKV_BLOCK), lambda i, j: (j, i, 0)),
            pl.BlockSpec((tm, LANES), lambda i, j: (i, 0)),
            kv_spec, kv_spec, kv_spec, kv_spec,
        ],
        out_shape=[
            jax.ShapeDtypeStruct((n_blocks, t, QKV_BLOCK), F32),
            jax.ShapeDtypeStruct((t, LANES), F32),
            kv_shape, kv_shape, kv_shape, kv_shape,
        ],
        scratch_shapes=[pltpu.VMEM((tm, d), BF16)],
        compiler_params=_cparams(("arbitrary", "arbitrary")),
        name="inproj",
    )(x, g, w_main, w_f, b_f)


def _cumsum_kernel(x_ref, c_ref, *, chunk):
    s = x_ref.shape[-1]
    r = lax.broadcasted_iota(I32, (chunk, chunk), 0)
    c = lax.broadcasted_iota(I32, (chunk, chunk), 1)
    upper = (r <= c).astype(BF16)
    carry = jnp.zeros((x_ref.shape[0], 1), F32)
    for k in range(s // chunk):
        x = x_ref[:, k * chunk:(k + 1) * chunk]
        stacked = jnp.concatenate(_split_terms(x, 3), axis=0).astype(BF16)
        parts = jnp.dot(stacked, upper, preferred_element_type=F32)
        n = x.shape[0]
        cs = parts[:n] + parts[n:2 * n] + parts[2 * n:] + carry
        c_ref[:, k * chunk:(k + 1) * chunk] = cs
        carry = cs[:, chunk - 1:chunk]


def _cumsum_rows(x):
    b, h, s = x.shape
    return pl.pallas_call(
        functools.partial(_cumsum_kernel, chunk=256),
        grid=(b,),
        in_specs=[pl.BlockSpec((None, h, s), lambda i: (i, 0, 0))],
        out_specs=pl.BlockSpec((None, h, s), lambda i: (i, 0, 0)),
        out_shape=jax.ShapeDtypeStruct((b, h, s), F32),
        compiler_params=_cparams(("arbitrary",)),
        name="logf_cumsum",
    )(x)


def _head_rows(ref, start, n, head, heads):
    del heads
    return ref[pl.ds(start, n), head * HEAD_DIM:(head + 1) * HEAD_DIM].astype(BF16)


def _sb_kernel(q_ref, k_ref, v_ref, o_ref, *, tq, tk, heads):
    i = pl.program_id(1)
    r = lax.broadcasted_iota(I32, (2 * tk, tk), 0)
    c = lax.broadcasted_iota(I32, (2 * tk, tk), 1)
    not_before2 = ((r & (tk - 1)) >= c).astype(BF16)
    strictly_causal = lax.broadcasted_iota(I32, (tq, tk), 1) < lax.broadcasted_iota(I32, (tq, tk), 0)
    cols = [slice(g * HEAD_DIM, (g + 1) * HEAD_DIM) for g in range(heads)]
    q = [(q_ref[:, cs] * (SCALE * LOG2E)).astype(BF16) for cs in cols]
    hs = range(heads)

    def block(j, carry, diag):
        out = ()
        for lo in range(0, heads, SB_HEAD_GROUP):
            out = out + group_block(j, carry[lo:lo + SB_HEAD_GROUP], diag, lo)
        return out

    def group_block(j, carry, diag, lo):
        ids = range(len(carry))
        start = pl.multiple_of(j * tk, tk)
        z = [_dot_nt(q[lo + n], _head_rows(k_ref, start, tk, lo + n, heads)) for n in ids]
        log_stay = [_log2_sigmoid_neg(z[n]) for n in ids]
        if diag:
            log_stay = [jnp.where(strictly_causal, x, 0.0) for x in log_stay]
        incl = [jnp.dot(_split_hi_lo_lanes(log_stay[n]), not_before2, preferred_element_type=F32) for n in ids]
        a = [jnp.exp2(z[n] + incl[n] + carry[n][0]) for n in ids]
        if diag:
            a = [jnp.where(strictly_causal, x, 0.0) for x in a]
        pv = [jnp.dot(a[n].astype(BF16), _head_rows(v_ref, start, tk, lo + n, heads),
                      preferred_element_type=F32) for n in ids]
        return tuple((carry[n][0] + incl[n][:, 0:1], carry[n][1] + pv[n]) for n in ids)

    init = tuple((jnp.zeros((tq, 1), F32), jnp.zeros((tq, HEAD_DIM), F32)) for _ in hs)
    carry = block(i, init, True)
    carry = lax.fori_loop(0, i, lambda t, cy: block(i - 1 - t, cy, False), carry)
    for g in range(heads):
        o_ref[:, cols[g]] = carry[g][1]


def _sb_attention(z, batch, seq, tq, tk):
    nq = seq // tq
    kv_spec = lambda which: pl.BlockSpec((None, seq, QKV_BLOCK), lambda b, i: (which, b, 0))
    return pl.pallas_call(
        functools.partial(_sb_kernel, tq=tq, tk=tk, heads=N_SB),
        grid=(batch, nq),
        in_specs=[pl.BlockSpec((None, tq, QKV_BLOCK), lambda b, i: (0, b * nq + i, 0)), kv_spec(1), kv_spec(2)],
        out_specs=pl.BlockSpec((tq, QKV_BLOCK), lambda b, i: (b * nq + i, 0)),
        out_shape=jax.ShapeDtypeStruct((batch * seq, QKV_BLOCK), F32),
        compiler_params=_cparams(("arbitrary", "arbitrary")),
        name="sb_attention",
    )(z, z, z)


def _fox_kernel(q_ref, k_ref, v_ref, c_ref, o_ref, *, tq, tk, heads):
    i = pl.program_id(1)
    causal = lax.broadcasted_iota(I32, (tq, tk), 1) <= lax.broadcasted_iota(I32, (tq, tk), 0)
    cols = [slice(g * HEAD_DIM, (g + 1) * HEAD_DIM) for g in range(heads)]
    q = [(q_ref[:, cs] * (SCALE * LOG2E)).astype(BF16) for cs in cols]
    q_start = pl.multiple_of(i * tq, tq)
    c_q0 = [c_ref[g, :, pl.ds(q_start, tq)][:, 0:1] for g in range(heads)]
    hs = range(heads)

    def block(j, carry, diag):
        out = ()
        for lo in range(0, heads, FOX_HEAD_GROUP):
            out = out + group_block(j, carry[lo:lo + FOX_HEAD_GROUP], diag, lo)
        return out

    def group_block(j, carry, diag, lo):
        ids = range(len(carry))
        start = pl.multiple_of(j * tk, tk)
        s = [_dot_nt(q[lo + n], _head_rows(k_ref, start, tk, lo + n, heads))
             + (c_q0[lo + n] - c_ref[lo + n, :, pl.ds(start, tk)]) * LOG2E for n in ids]
        if diag:
            s = [jnp.where(causal, x, NEG_INF) for x in s]
        m_new = [jnp.maximum(carry[n][0], jnp.max(s[n], axis=-1, keepdims=True)) for n in ids]
        p = [jnp.exp2(s[n] - m_new[n]) for n in ids]
        alpha = [jnp.exp2(carry[n][0] - m_new[n]) for n in ids]
        pv = [jnp.dot(p[n].astype(BF16), _head_rows(v_ref, start, tk, lo + n, heads),
                      preferred_element_type=F32) for n in ids]
        return tuple((m_new[n], alpha[n] * carry[n][1] + jnp.sum(p[n], axis=-1, keepdims=True),
                      alpha[n] * carry[n][2] + pv[n]) for n in ids)

    init = tuple((jnp.full((tq, 1), NEG_INF, F32), jnp.zeros((tq, 1), F32), jnp.zeros((tq, HEAD_DIM), F32))
                 for _ in hs)
    carry = block(i, init, True)
    carry = lax.fori_loop(0, i, lambda t, cy: block(i - 1 - t, cy, False), carry)
    for g in range(heads):
        o_ref[:, cols[g]] = carry[g][2] / carry[g][1]


def _fox_attention(z, c_rows, batch, seq, tq, tk):
    nq = seq // tq
    kv_spec = lambda which: pl.BlockSpec((None, seq, QKV_BLOCK), lambda b, i: (which, b, 0))
    return pl.pallas_call(
        functools.partial(_fox_kernel, tq=tq, tk=tk, heads=N_FOX),
        grid=(batch, nq),
        in_specs=[pl.BlockSpec((None, tq, QKV_BLOCK), lambda b, i: (3, b * nq + i, 0)), kv_spec(4), kv_spec(5),
                  pl.BlockSpec((N_FOX, 1, seq), lambda b, i: (b, 0, 0))],
        out_specs=pl.BlockSpec((tq, QKV_BLOCK), lambda b, i: (b * nq + i, 0)),
        out_shape=jax.ShapeDtypeStruct((batch * seq, QKV_BLOCK), F32),
        compiler_params=_cparams(("arbitrary", "arbitrary")),
        name="fox_attention",
    )(z, z, z, c_rows)


def _reverse_cumsum_lanes(x, lane):
    n = x.shape[-1]
    d = N_SB
    while d < n:
        shifted = pltpu.roll(x, n - d, 1)
        x = x + jnp.where(lane + d < n, shifted, 0.0)
        d *= 2
    return x


def _decode_kernel(pt_ref, qs_ref, qf_ref, kown_ref, vown_ref, lfown_ref, *refs, pages_per_step):
    del pt_ref
    pp = pages_per_step
    ksb = refs[0:pp]
    vsb = refs[pp:2 * pp]
    kfx = refs[2 * pp:3 * pp]
    vfx = refs[3 * pp:4 * pp]
    lfc = refs[4 * pp:5 * pp]
    osb_ref, ofx_ref = refs[5 * pp:5 * pp + 2]
    acc_sb, acc_fx, run_sb, run_fx, m_ref, l_ref = refs[5 * pp + 2:]
    j = pl.program_id(1)
    page_rows = ksb[0].shape[0] * ksb[0].shape[1]
    rows = pp * page_rows

    qs = qs_ref[...] * SCALE
    qf = qf_ref[...] * SCALE

    @pl.when(j == 0)
    def _():
        acc_sb[...] = jnp.zeros_like(acc_sb)
        run_sb[...] = jnp.zeros_like(run_sb)
        m_ref[...] = jnp.sum(qf * kown_ref[...], axis=-1, keepdims=True)
        l_ref[...] = jnp.ones_like(l_ref)
        acc_fx[...] = vown_ref[...]
        run_fx[...] = lfown_ref[...]

    def rows_of(page_refs):
        return jnp.concatenate([page_refs[p][...].reshape(page_rows, HEAD_DIM).astype(BF16)
                                for p in reversed(range(pp))], axis=0)

    lane = lax.broadcasted_iota(I32, (N_SB, rows), 1)
    head = lax.broadcasted_iota(I32, (N_SB, rows), 0)
    valid = (lane & (N_SB - 1)) == head

    z = _dot_nt(qs.astype(BF16), rows_of(ksb))
    log_stay = jnp.where(valid, _log_sigmoid(-z), 0.0)
    incl = _reverse_cumsum_lanes(log_stay, lane)
    run = run_sb[...]
    a = jnp.where(valid, jnp.exp(z + incl + run), 0.0)
    acc_sb[...] += jnp.dot(a.astype(BF16), rows_of(vsb), preferred_element_type=F32)
    run_sb[...] = run + jnp.sum(log_stay, axis=-1, keepdims=True)

    lf = jnp.where(valid, jnp.concatenate([lfc[p][...] for p in reversed(range(pp))], axis=1), 0.0)
    incl = _reverse_cumsum_lanes(lf, lane)
    run = run_fx[...]
    s = jnp.where(valid, _dot_nt(qf.astype(BF16), rows_of(kfx)) + (incl - lf) + run, NEG_INF)
    m_old = m_ref[...]
    m_new = jnp.maximum(m_old, jnp.max(s, axis=-1, keepdims=True))
    alpha = jnp.exp(m_old - m_new)
    pr = jnp.exp(s - m_new)
    l_ref[...] = alpha * l_ref[...] + jnp.sum(pr, axis=-1, keepdims=True)
    acc_fx[...] = alpha * acc_fx[...] + jnp.dot(pr.astype(BF16), rows_of(vfx), preferred_element_type=F32)
    m_ref[...] = m_new
    run_fx[...] = run + jnp.sum(lf, axis=-1, keepdims=True)

    @pl.when(j == pl.num_programs(1) - 1)
    def _():
        osb_ref[...] = acc_sb[...]
        ofx_ref[...] = acc_fx[...] / l_ref[...]


def _decode_attention(page_table, q_sb, q_fx, k_own, v_own, lf_own, c_sb_k, c_sb_v, c_fx_k, c_fx_v,
                      c_logf_rows, pages_per_step):
    nb, n_pages = page_table.shape
    page, heads, hd = c_sb_k.shape[1:]
    rows = page * heads
    pp = pages_per_step
    steps = n_pages // pp

    def page_of(p):
        return lambda b, j, pt: (pt[b * n_pages + (n_pages - 1 - (j * pp + p))], 0, 0, 0)

    def lf_page_of(p):
        return lambda b, j, pt: (pt[b * n_pages + (n_pages - 1 - (j * pp + p))], 0, 0)

    per_b = pl.BlockSpec((None, heads, hd), lambda b, j, pt: (b, 0, 0))
    cache_specs = [pl.BlockSpec((None, page, heads, hd), page_of(p)) for p in range(pp)]
    lf_specs = [pl.BlockSpec((None, 1, rows), lf_page_of(p)) for p in range(pp)]
    grid_spec = pltpu.PrefetchScalarGridSpec(
        num_scalar_prefetch=1,
        grid=(nb, steps),
        in_specs=[per_b, per_b, per_b, per_b, pl.BlockSpec((None, heads, 1), lambda b, j, pt: (b, 0, 0))]
        + cache_specs * 4 + lf_specs,
        out_specs=[per_b, per_b],
        scratch_shapes=[pltpu.VMEM((heads, hd), F32), pltpu.VMEM((heads, hd), F32),
                        pltpu.VMEM((heads, 1), F32), pltpu.VMEM((heads, 1), F32),
                        pltpu.VMEM((heads, 1), F32), pltpu.VMEM((heads, 1), F32)],
    )
    return pl.pallas_call(
        functools.partial(_decode_kernel, pages_per_step=pp),
        grid_spec=grid_spec,
        out_shape=[jax.ShapeDtypeStruct((nb, heads, hd), F32)] * 2,
        compiler_params=_cparams(("arbitrary", "arbitrary")),
        name="decode_attention",
    )(page_table.reshape(-1), q_sb, q_fx, k_own, v_own, lf_own,
      *([c_sb_k] * pp), *([c_sb_v] * pp), *([c_fx_k] * pp), *([c_fx_v] * pp), *([c_logf_rows] * pp))


ROUTE_EXPERT, ROUTE_GATE, ROUTE_RANK = 0, 2, 4


def _merge_kernel(osb_ref, ofx_ref, x_ref, gsb_ref, gfx_ref, wout_ref, gffn_ref, wr_ref, br_ref,
                  hp_ref, u_ref, route_ref, count_ref, base_ref):
    @pl.when(pl.program_id(0) == 0)
    def _():
        base_ref[...] = jnp.zeros_like(base_ref)

    o = jnp.concatenate([_rms(osb_ref[...], gsb_ref[...]), _rms(ofx_ref[...], gfx_ref[...])], axis=-1)
    y = jnp.dot(o.astype(BF16), wout_ref[...], preferred_element_type=F32)
    hp = x_ref[...] + y
    hp_ref[...] = hp
    u = _rms(hp, gffn_ref[...])
    tm = u.shape[0]
    for s in range(SLAB_ROWS):
        u_ref[s // HALF_ROWS, pl.ds(s % HALF_ROWS, tm, stride=HALF_ROWS), :] = u[:, s * LANES:(s + 1) * LANES]

    stacked = jnp.concatenate(_split_terms(u), axis=0).astype(BF16)
    parts = jnp.dot(stacked, wr_ref[...], preferred_element_type=F32)
    both = parts[:tm] + parts[tm:]
    logits = both[:, :LANES] + both[:, LANES:] + br_ref[...]

    lane = lax.broadcasted_iota(I32, (tm, LANES), 1).astype(F32)
    is_group = (lane >= N_EXPERTS) & (lane < N_EXPERTS + N_GROUPS)
    big = float(1 << 20)
    lg = jnp.where(is_group, logits, NEG_INF)
    lg_max = jnp.max(lg, axis=-1, keepdims=True)
    gsel = jnp.min(jnp.where(lg == lg_max, lane, big), axis=-1, keepdims=True) - N_EXPERTS
    g1 = 1.0 / jnp.sum(jnp.exp(lg - lg_max), axis=-1, keepdims=True)

    in_group = (lane >= gsel * EXPERTS_PER_GROUP) & (lane < (gsel + 1) * EXPERTS_PER_GROUP)
    le = jnp.where(in_group, logits, NEG_INF)
    v_a = jnp.max(le, axis=-1, keepdims=True)
    i_a = jnp.min(jnp.where(le == v_a, lane, big), axis=-1, keepdims=True)
    le_b = jnp.where(lane == i_a, NEG_INF, le)
    v_b = jnp.max(le_b, axis=-1, keepdims=True)
    i_b = jnp.min(jnp.where(le_b == v_b, lane, big), axis=-1, keepdims=True)
    e_b = jnp.exp(v_b - v_a)
    gate_a = g1 / (1.0 + e_b)
    gate_b = g1 * e_b / (1.0 + e_b)

    onehot = jnp.where((lane == i_a) | (lane == i_b), 1.0, 0.0)
    rr = lax.broadcasted_iota(I32, (tm, tm), 0)
    cc = lax.broadcasted_iota(I32, (tm, tm), 1)
    count_dtype = BF16 if tm % 16 == 0 else F32
    earlier = (cc < rr).astype(count_dtype)
    rank_all = jnp.dot(earlier, onehot.astype(count_dtype), preferred_element_type=F32) + base_ref[...]
    rank_a = jnp.sum(jnp.where(lane == i_a, rank_all, 0.0), axis=-1, keepdims=True)
    rank_b = jnp.sum(jnp.where(lane == i_b, rank_all, 0.0), axis=-1, keepdims=True)
    total = base_ref[...] + jnp.sum(onehot, axis=0, keepdims=True)
    base_ref[...] = total
    count_ref[...] = total

    route = jnp.zeros((tm, LANES), F32)
    for k, val in ((ROUTE_EXPERT, i_a), (ROUTE_EXPERT + 1, i_b), (ROUTE_GATE, gate_a), (ROUTE_GATE + 1, gate_b),
                   (ROUTE_RANK, rank_a), (ROUTE_RANK + 1, rank_b)):
        route = jnp.where(lane == k, val, route)
    route_ref[...] = route


def _merge(o_sb, o_fx, x, g_sb, g_fx, w_out, g_ffn, w_r, b_r, tm):
    t, d = x.shape
    row = lambda n: pl.BlockSpec((tm, n), lambda i: (i, 0))
    const = lambda a: pl.BlockSpec(a.shape, lambda i: (0,) * a.ndim)
    return pl.pallas_call(
        _merge_kernel,
        grid=(t // tm,),
        in_specs=[row(o_sb.shape[1]), row(o_fx.shape[1]), row(d), const(g_sb), const(g_fx), const(w_out),
                  const(g_ffn), const(w_r), const(b_r)],
        out_specs=[row(d), pl.BlockSpec((2, tm * HALF_ROWS, LANES), lambda i: (0, i, 0)), row(LANES),
                   pl.BlockSpec((1, LANES), lambda i: (0, 0))],
        out_shape=[jax.ShapeDtypeStruct((t, d), F32), jax.ShapeDtypeStruct((2, t * HALF_ROWS, LANES), F32),
                   jax.ShapeDtypeStruct((t, LANES), F32), jax.ShapeDtypeStruct((1, LANES), F32)],
        scratch_shapes=[pltpu.VMEM((1, LANES), F32)],
        compiler_params=_cparams(("arbitrary",)),
        name="merge_router",
    )(o_sb, o_fx, x, g_sb, g_fx, w_out, g_ffn, w_r, b_r)


def _dispatch_plan(route, counts, tile, n_tiles_max):
    t = route.shape[0]
    expert = route[:, ROUTE_EXPERT:ROUTE_EXPERT + TOP_K].T.astype(I32)
    rank = route[:, ROUTE_RANK:ROUTE_RANK + TOP_K].T.astype(I32)
    count = counts[0, :N_EXPERTS].astype(I32)
    tiles = (count + tile - 1) // tile
    tile_end = jnp.cumsum(tiles)
    first_row = ((tile_end - tiles) * tile).astype(F32)
    row0 = jnp.dot(jax.nn.one_hot(expert, N_EXPERTS, dtype=F32), first_row, precision=lax.Precision.HIGHEST)
    pos = (row0.astype(I32) + rank).reshape(-1)
    tile_ids = jnp.arange(n_tiles_max, dtype=I32)
    tile_expert = jnp.minimum(jnp.sum((tile_end[None, :] <= tile_ids[:, None]).astype(I32), axis=1), N_EXPERTS - 1)
    token = jnp.tile(jnp.arange(t, dtype=I32), TOP_K)
    row_token = jnp.zeros((n_tiles_max * tile,), I32).at[pos].set(token, unique_indices=True)
    return pos, tile_expert, row_token, tile_end[-1:].astype(I32)


def _slab_copy(src_hbm, src_row, dst, dst_row, sem):
    return pltpu.make_async_copy(src_hbm.at[pl.ds(pl.multiple_of(src_row * SLAB_ROWS, SLAB_ROWS), SLAB_ROWS), :],
                                 dst.at[pl.ds(pl.multiple_of(dst_row * SLAB_ROWS, SLAB_ROWS), SLAB_ROWS), :], sem)


def _rows_from_slabs(slabs, n):
    return jnp.concatenate([slabs[pl.ds(s, n, stride=SLAB_ROWS), :] for s in range(SLAB_ROWS)], axis=1)


def _expert_kernel(te_ref, rt_ref, nt_ref, u_hbm, wg_ref, wu_ref, *refs, tile, half):
    final = half == 1
    if final:
        wd_ref, part_ref, out_ref, u_vmem, xs, sem, wg_b, wu_b, wd_b = refs
    else:
        out_ref, u_vmem, xs, sem, wg_b, wu_b = refs
    i = pl.program_id(0)
    nt = nt_ref[0]
    d_e = wg_b.shape[1]

    @pl.when(i == 0)
    def _():
        load = pltpu.make_async_copy(u_hbm.at[half], u_vmem, sem.at[0])
        load.start()
        load.wait()

    @pl.when(i < nt)
    def _():
        @pl.when((i == 0) | (te_ref[i] != te_ref[jnp.maximum(i - 1, 0)]))
        def _():
            wg_b[...] = wg_ref[...].astype(BF16)
            wu_b[...] = wu_ref[...].astype(BF16)
            if final:
                wd_b[...] = wd_ref[...].astype(BF16)

        base = i * tile

        def body(r, carry):
            src = pl.multiple_of(rt_ref[base + r] * HALF_ROWS, HALF_ROWS)
            xs[pl.ds(pl.multiple_of(r * HALF_ROWS, HALF_ROWS), HALF_ROWS), :] = u_vmem[pl.ds(src, HALF_ROWS), :]
            return carry

        lax.fori_loop(0, tile, body, 0, unroll=8)
        x = jnp.concatenate([xs[pl.ds(s, tile, stride=HALF_ROWS), :] for s in range(HALF_ROWS)],
                            axis=1).astype(BF16)
        gate = jnp.dot(x, wg_b[...], preferred_element_type=F32)
        up = jnp.dot(x, wu_b[...], preferred_element_type=F32)
        if final:
            gate = gate + part_ref[:, :d_e]
            up = up + part_ref[:, d_e:]
            h = gate * (1.0 / (1.0 + jnp.exp(-gate))) * up
            y = jnp.dot(h.astype(BF16), wd_b[...], preferred_element_type=F32)
            for s in range(SLAB_ROWS):
                out_ref[pl.ds(s, tile, stride=SLAB_ROWS), :] = y[:, s * LANES:(s + 1) * LANES]
        else:
            out_ref[:, :d_e] = gate
            out_ref[:, d_e:] = up

    @pl.when(i >= nt)
    def _():
        out_ref[...] = jnp.zeros_like(out_ref)


def _experts(u_halves, tile_expert, row_token, n_tiles, w_gate, w_up, w_down, tile):
    n_e, d, d_e = w_gate.shape
    half_d = HALF_ROWS * LANES
    n_tiles_max = tile_expert.shape[0]
    last = lambda i, nt: jnp.minimum(i, nt[0] - 1)

    def call(half, extra_in, extra_specs, out_spec, out_shape, extra_scratch):
        grid_spec = pltpu.PrefetchScalarGridSpec(
            num_scalar_prefetch=3,
            grid=(n_tiles_max,),
            in_specs=[
                pl.BlockSpec(memory_space=pl.ANY),
                pl.BlockSpec((None, half_d, d_e), lambda i, te, rt, nt: (te[last(i, nt)], half, 0)),
                pl.BlockSpec((None, half_d, d_e), lambda i, te, rt, nt: (te[last(i, nt)], half, 0)),
            ] + extra_specs,
            out_specs=out_spec,
            scratch_shapes=[pltpu.VMEM(u_halves.shape[1:], F32), pltpu.VMEM((tile * HALF_ROWS, LANES), F32),
                            pltpu.SemaphoreType.DMA((1,)),
                            pltpu.VMEM((half_d, d_e), BF16), pltpu.VMEM((half_d, d_e), BF16)] + extra_scratch,
        )
        return pl.pallas_call(
            functools.partial(_expert_kernel, tile=tile, half=half),
            grid_spec=grid_spec,
            out_shape=out_shape,
            compiler_params=_cparams(("arbitrary",)),
            name=f"moe_experts_half{half}",
        )(tile_expert, row_token, n_tiles, u_halves, w_gate, w_up, *extra_in)

    part_spec = pl.BlockSpec((tile, 2 * d_e), lambda i, te, rt, nt: (i, 0))
    part = call(0, [], [], part_spec, jax.ShapeDtypeStruct((n_tiles_max * tile, 2 * d_e), F32), [])
    return call(1, [w_down, part],
                [pl.BlockSpec((None, d_e, d), lambda i, te, rt, nt: (te[last(i, nt)], 0, 0)), part_spec],
                pl.BlockSpec((tile * SLAB_ROWS, LANES), lambda i, te, rt, nt: (i, 0)),
                jax.ShapeDtypeStruct((n_tiles_max * tile * SLAB_ROWS, LANES), F32),
                [pltpu.VMEM((d_e, d), BF16)])


def _combine_kernel(pos_ref, rows_hbm, hp_ref, route_ref, gfin_ref, y_ref, buf, sem, *, tm):
    i = pl.program_id(0)
    n = pl.num_programs(0)
    slot = lax.rem(i, 2)

    def gather(tile_idx, dst_slot):
        base = tile_idx * tm

        def body(r, carry):
            for k in range(TOP_K):
                _slab_copy(rows_hbm, pos_ref[k * (n * tm) + base + r], buf.at[dst_slot, k], r,
                           sem.at[dst_slot]).start()
            return carry

        lax.fori_loop(0, tm, body, 0, unroll=4)

    @pl.when(i == 0)
    def _():
        gather(0, 0)

    @pl.when(i + 1 < n)
    def _():
        gather(i + 1, 1 - slot)

    pltpu.make_async_copy(buf.at[slot], buf.at[slot], sem.at[slot]).wait()
    route = route_ref[...]
    g_a = route[:, ROUTE_GATE:ROUTE_GATE + 1]
    g_b = route[:, ROUTE_GATE + 1:ROUTE_GATE + 2]
    f = g_a * _rows_from_slabs(buf.at[slot, 0], tm) + g_b * _rows_from_slabs(buf.at[slot, 1], tm)
    y_ref[...] = _rms(hp_ref[...] + f, gfin_ref[...])


def _combine(pos, rows, hp, route, g_final, tm):
    t, d = hp.shape
    grid_spec = pltpu.PrefetchScalarGridSpec(
        num_scalar_prefetch=1,
        grid=(t // tm,),
        in_specs=[
            pl.BlockSpec(memory_space=pl.ANY),
            pl.BlockSpec((tm, d), lambda i, pos: (i, 0)),
            pl.BlockSpec((tm, LANES), lambda i, pos: (i, 0)),
            pl.BlockSpec((1, d), lambda i, pos: (0, 0)),
        ],
        out_specs=pl.BlockSpec((tm, d), lambda i, pos: (i, 0)),
        scratch_shapes=[pltpu.VMEM((2, TOP_K, tm * SLAB_ROWS, LANES), F32), pltpu.SemaphoreType.DMA((2,))],
    )
    return pl.pallas_call(
        functools.partial(_combine_kernel, tm=tm),
        grid_spec=grid_spec,
        out_shape=jax.ShapeDtypeStruct((t, d), F32),
        compiler_params=_cparams(("arbitrary",)),
        name="moe_combine",
    )(pos, rows, hp, route, g_final)


def _routed_moe(u, hp, route, counts, w_gate, w_up, w_down, g_final, tile, tm_combine):
    t = hp.shape[0]
    n_tiles_max = (TOP_K * t + N_EXPERTS * (tile - 1)) // tile
    pos, tile_expert, row_token, n_tiles = _dispatch_plan(route, counts, tile, n_tiles_max)
    rows = _experts(u, tile_expert, row_token, n_tiles, w_gate, w_up, w_down, tile)
    return _combine(pos, rows, hp, route, g_final, tm_combine)


def _pad_lanes(a, n=LANES):
    return jnp.pad(a, ((0, 0), (0, n - a.shape[1])))


def kernel(x_prompt, x_sample, cache_sb_k, cache_sb_v, cache_fox_k, cache_fox_v, cache_fox_logf, page_table, norm_attn_g, w_in, b_forget, g_sb_out, g_fox_out, w_out, norm_ffn_g, w_router_group, b_router_group, w_router_expert, b_router_expert, w_expert_gate, w_expert_up, w_expert_down, norm_final_g):
    batch, seq, d = x_prompt.shape
    nb = x_sample.shape[0]
    assert w_in.shape[0] == 1, "one trunk layer"
    d_qkv = 3 * (N_SB + N_FOX) * HEAD_DIM

    w_main = w_in[0, :, :d_qkv].astype(BF16)
    w_f = _pad_lanes(w_in[0, :, d_qkv:]).astype(BF16)
    b_f = _pad_lanes(b_forget)
    g_attn = norm_attn_g
    w_out_b = w_out[0].astype(BF16)
    w_r = jnp.concatenate([w_router_expert[0].transpose(1, 0, 2).reshape(d, N_EXPERTS), w_router_group[0]], axis=1)
    w_r = _pad_lanes(w_r)
    w_r_hi = w_r.astype(BF16)
    w_r_lo = (w_r - w_r_hi.astype(F32)).astype(BF16)
    w_r2 = jnp.concatenate([w_r_hi, w_r_lo], axis=1)
    b_r = _pad_lanes(jnp.concatenate([b_router_expert[0].reshape(1, N_EXPERTS), b_router_group], axis=1))
    g_fin = norm_final_g.reshape(1, d)
    experts = (w_expert_gate[0], w_expert_up[0], w_expert_down[0])

    xp = x_prompt.reshape(batch * seq, d)
    z_p, logf_p, *kv_p = _inproj(xp, g_attn, w_main, w_f, b_f, tm=TM_INPROJ)
    logf_p = logf_p[:, :N_FOX].reshape(batch, seq, N_FOX)
    c_rows = _cumsum_rows(logf_p.transpose(0, 2, 1)).reshape(batch * N_FOX, 1, seq)
    o_sb = _sb_attention(z_p, batch, seq, TQ, TK)
    o_fx = _fox_attention(z_p, c_rows, batch, seq, TQ, TK)
    hp_p, u_p, route_p, count_p = _merge(o_sb, o_fx, xp, g_sb_out, g_fox_out, w_out_b, norm_ffn_g, w_r2, b_r,
                                         tm=TM_MERGE)
    y_prompt = _routed_moe(u_p, hp_p, route_p, count_p, *experts, g_fin, TILE_EXPERT, TM_COMBINE)
    y_prompt = y_prompt.reshape(batch, seq, d)

    xs = x_sample.reshape(nb, d)
    z_s, logf_s, *kv_s = _inproj(xs, g_attn, w_main, w_f, b_f, tm=nb)
    logf_s = logf_s[:, :N_FOX]
    per_head = lambda a: a.reshape(nb, N_SB, HEAD_DIM)
    page = cache_fox_logf.shape[2]
    logf_rows = cache_fox_logf[0].reshape(-1, 1, page * N_FOX)
    o_sb_s, o_fx_s = _decode_attention(
        page_table, per_head(z_s[0]), per_head(z_s[3]), per_head(kv_s[2]), per_head(kv_s[3]),
        logf_s.reshape(nb, N_FOX, 1), cache_sb_k[0], cache_sb_v[0], cache_fox_k[0], cache_fox_v[0],
        logf_rows, PAGES_PER_STEP)
    hp_s, u_s, route_s, count_s = _merge(o_sb_s.reshape(nb, -1), o_fx_s.reshape(nb, -1), xs, g_sb_out, g_fox_out,
                                         w_out_b, norm_ffn_g, w_r2, b_r, tm=nb)
    y_sample = _routed_moe(u_s, hp_s, route_s, count_s, *experts, g_fin, nb, nb).reshape(nb, 1, d)

    heads_p = lambda a: a.reshape(1, batch, seq, N_SB, HEAD_DIM)
    heads_s = lambda a: a.reshape(1, nb, 1, N_SB, HEAD_DIM)
    return (y_prompt, y_sample,
            heads_p(kv_p[0]), heads_p(kv_p[1]), heads_p(kv_p[2]), heads_p(kv_p[3]), logf_p[None],
            heads_s(kv_s[0]), heads_s(kv_s[1]), heads_s(kv_s[2]), heads_s(kv_s[3]),
            logf_s.reshape(1, nb, 1, N_FOX))
```

```python
import functools

import jax
import jax.numpy as jnp
from jax import lax
from jax.experimental import pallas as pl
from jax.experimental.pallas import tpu as pltpu

HEAD_DIM = 128
N_SB = 8
N_FOX = 8
N_GROUPS = 4
EXPERTS_PER_GROUP = 8
N_EXPERTS = N_GROUPS * EXPERTS_PER_GROUP
TOP_K = 2
SCALE = HEAD_DIM ** -0.5
LOG2E = 1.4426950408889634
RMS_EPS = 1e-6
LANES = 128
D_MODEL = 2048
QKV_BLOCK = N_SB * HEAD_DIM
SLAB_ROWS = D_MODEL // LANES
HALF_ROWS = SLAB_ROWS // 2
VMEM_LIMIT = 56 * 1024 * 1024
NEG_INF = float("-inf")
BF16 = jnp.bfloat16
F32 = jnp.float32
I32 = jnp.int32

TM_INPROJ = 512
TQ = 256
TK = 256
SB_HEAD_GROUP = 4
FOX_HEAD_GROUP = 4
PAGES_PER_STEP = 8
TM_MERGE = 512
TILE_EXPERT = 512
TM_COMBINE = 256


def _cparams(sem):
    return pltpu.CompilerParams(dimension_semantics=sem, vmem_limit_bytes=VMEM_LIMIT)


def _rms(x, g):
    return x * lax.rsqrt(jnp.mean(x * x, axis=-1, keepdims=True) + RMS_EPS) * g


def _log_sigmoid(x):
    return jnp.minimum(x, 0.0) - jnp.log(1.0 + jnp.exp(-jnp.abs(x)))


def _split_terms(x, terms=2):
    pieces = []
    rest = x
    for _ in range(terms - 1):
        head = rest.astype(BF16).astype(F32)
        pieces.append(head)
        rest = rest - head
    pieces.append(rest)
    return pieces


def _log2_sigmoid_neg(z2):
    return jnp.log(1.0 + jnp.exp2(-jnp.abs(z2))) * (-LOG2E) - jnp.maximum(z2, 0.0)


def _split_hi_lo_lanes(x):
    hi = x.astype(BF16).astype(F32)
    return jnp.concatenate([hi, x - hi], axis=1).astype(BF16)


def _dot_nt(a, b):
    return lax.dot_general(a, b, (((1,), (1,)), ((), ())), preferred_element_type=F32)


def _inproj_kernel(x_ref, g_ref, w_ref, wf_ref, bf_ref, z_ref, logf_ref, ksb_ref, vsb_ref, kfx_ref, vfx_ref,
                   u_ref):
    j = pl.program_id(1)
    tm = x_ref.shape[0]

    @pl.when(j == 0)
    def _():
        u = _rms(x_ref[...], g_ref[...]).astype(BF16)
        u_ref[...] = u
        f = jnp.dot(u, wf_ref[...], preferred_element_type=F32) + bf_ref[...]
        logf_ref[...] = _log_sigmoid(f)

    z_ref[...] = jnp.dot(u_ref[...], w_ref[...], preferred_element_type=F32)

    for block, ref in ((1, ksb_ref), (2, vsb_ref), (4, kfx_ref), (5, vfx_ref)):
        @pl.when(j == block)
        def _(ref=ref):
            for h in range(N_SB):
                ref[pl.ds(h, tm, stride=N_SB), :] = z_ref[:, h * HEAD_DIM:(h + 1) * HEAD_DIM]


def _inproj(x, g, w_main, w_f, b_f, tm):
    t, d = x.shape
    n_blocks = w_main.shape[1] // QKV_BLOCK
    kv_spec = pl.BlockSpec((tm * N_SB, HEAD_DIM), lambda i, j: (i, 0))
    kv_shape = jax.ShapeDtypeStruct((t * N_SB, HEAD_DIM), F32)
    return pl.pallas_call(
        _inproj_kernel,
        grid=(t // tm, n_blocks),
        in_specs=[
            pl.BlockSpec((tm, d), lambda i, j: (i, 0)),
            pl.BlockSpec((1, d), lambda i, j: (0, 0)),
            pl.BlockSpec((d, QKV_BLOCK), lambda i, j: (0, j)),
            pl.BlockSpec((d, LANES), lambda i, j: (0, 0)),
            pl.BlockSpec((1, LANES), lambda i, j: (0, 0)),
        ],
        out_specs=[
            pl.BlockSpec((None, tm, QKV_BLOCK), lambda i, j: (j, i, 0)),
            pl.BlockSpec((tm, LANES), lambda i, j: (i, 0)),
            kv_spec, kv_spec, kv_spec, kv_spec,
        ],
        out_shape=[
            jax.ShapeDtypeStruct((n_blocks, t, QKV_BLOCK), F32),
            jax.ShapeDtypeStruct((t, LANES), F32),
            kv_shape, kv_shape, kv_shape, kv_shape,
        ],
        scratch_shapes=[pltpu.VMEM((tm, d), BF16)],
        compiler_params=_cparams(("arbitrary", "arbitrary")),
        name="inproj",
    )(x, g, w_main, w_f, b_f)


def _cumsum_kernel(x_ref, c_ref, *, chunk):
    s = x_ref.shape[-1]
    r = lax.broadcasted_iota(I32, (chunk, chunk), 0)
    c = lax.broadcasted_iota(I32, (chunk, chunk), 1)
    upper = (r <= c).astype(BF16)
    carry = jnp.zeros((x_ref.shape[0], 1), F32)
    for k in range(s // chunk):
        x = x_ref[:, k * chunk:(k + 1) * chunk]
        stacked = jnp.concatenate(_split_terms(x, 3), axis=0).astype(BF16)
        parts = jnp.dot(stacked, upper, preferred_element_type=F32)
        n = x.shape[0]
        cs = parts[:n] + parts[n:2 * n] + parts[2 * n:] + carry
        c_ref[:, k * chunk:(k + 1) * chunk] = cs
        carry = cs[:, chunk - 1:chunk]


def _cumsum_rows(x):
    b, h, s = x.shape
    return pl.pallas_call(
        functools.partial(_cumsum_kernel, chunk=256),
        grid=(b,),
        in_specs=[pl.BlockSpec((None, h, s), lambda i: (i, 0, 0))],
        out_specs=pl.BlockSpec((None, h, s), lambda i: (i, 0, 0)),
        out_shape=jax.ShapeDtypeStruct((b, h, s), F32),
        compiler_params=_cparams(("arbitrary",)),
        name="logf_cumsum",
    )(x)


def _head_rows(ref, start, n, head, heads):
    del heads
    return ref[pl.ds(start, n), head * HEAD_DIM:(head + 1) * HEAD_DIM].astype(BF16)


def _sb_kernel(q_ref, k_ref, v_ref, o_ref, *, tq, tk, heads):
    i = pl.program_id(1)
    r = lax.broadcasted_iota(I32, (2 * tk, tk), 0)
    c = lax.broadcasted_iota(I32, (2 * tk, tk), 1)
    not_before2 = ((r & (tk - 1)) >= c).astype(BF16)
    strictly_causal = lax.broadcasted_iota(I32, (tq, tk), 1) < lax.broadcasted_iota(I32, (tq, tk), 0)
    cols = [slice(g * HEAD_DIM, (g + 1) * HEAD_DIM) for g in range(heads)]
    q = [(q_ref[:, cs] * (SCALE * LOG2E)).astype(BF16) for cs in cols]
    hs = range(heads)

    def block(j, carry, diag):
        out = ()
        for lo in range(0, heads, SB_HEAD_GROUP):
            out = out + group_block(j, carry[lo:lo + SB_HEAD_GROUP], diag, lo)
        return out

    def group_block(j, carry, diag, lo):
        ids = range(len(carry))
        start = pl.multiple_of(j * tk, tk)
        z = [_dot_nt(q[lo + n], _head_rows(k_ref, start, tk, lo + n, heads)) for n in ids]
        log_stay = [_log2_sigmoid_neg(z[n]) for n in ids]
        if diag:
            log_stay = [jnp.where(strictly_causal, x, 0.0) for x in log_stay]
        incl = [jnp.dot(_split_hi_lo_lanes(log_stay[n]), not_before2, preferred_element_type=F32) for n in ids]
        a = [jnp.exp2(z[n] + incl[n] + carry[n][0]) for n in ids]
        if diag:
            a = [jnp.where(strictly_causal, x, 0.0) for x in a]
        pv = [jnp.dot(a[n].astype(BF16), _head_rows(v_ref, start, tk, lo + n, heads),
                      preferred_element_type=F32) for n in ids]
        return tuple((carry[n][0] + incl[n][:, 0:1], carry[n][1] + pv[n]) for n in ids)

    init = tuple((jnp.zeros((tq, 1), F32), jnp.zeros((tq, HEAD_DIM), F32)) for _ in hs)
    carry = block(i, init, True)
    carry = lax.fori_loop(0, i, lambda t, cy: block(i - 1 - t, cy, False), carry)
    for g in range(heads):
        o_ref[:, cols[g]] = carry[g][1]


def _sb_attention(z, batch, seq, tq, tk):
    nq = seq // tq
    kv_spec = lambda which: pl.BlockSpec((None, seq, QKV_BLOCK), lambda b, i: (which, b, 0))
    return pl.pallas_call(
        functools.partial(_sb_kernel, tq=tq, tk=tk, heads=N_SB),
        grid=(batch, nq),
        in_specs=[pl.BlockSpec((None, tq, QKV_BLOCK), lambda b, i: (0, b * nq + i, 0)), kv_spec(1), kv_spec(2)],
        out_specs=pl.BlockSpec((tq, QKV_BLOCK), lambda b, i: (b * nq + i, 0)),
        out_shape=jax.ShapeDtypeStruct((batch * seq, QKV_BLOCK), F32),
        compiler_params=_cparams(("arbitrary", "arbitrary")),
        name="sb_attention",
    )(z, z, z)


def _fox_kernel(q_ref, k_ref, v_ref, c_ref, o_ref, *, tq, tk, heads):
    i = pl.program_id(1)
    causal = lax.broadcasted_iota(I32, (tq, tk), 1) <= lax.broadcasted_iota(I32, (tq, tk), 0)
    cols = [slice(g * HEAD_DIM, (g + 1) * HEAD_DIM) for g in range(heads)]
    q = [(q_ref[:, cs] * (SCALE * LOG2E)).astype(BF16) for cs in cols]
    q_start = pl.multiple_of(i * tq, tq)
    c_q0 = [c_ref[g, :, pl.ds(q_start, tq)][:, 0:1] for g in range(heads)]
    hs = range(heads)

    def block(j, carry, diag):
        out = ()
        for lo in range(0, heads, FOX_HEAD_GROUP):
            out = out + group_block(j, carry[lo:lo + FOX_HEAD_GROUP], diag, lo)
        return out

    def group_block(j, carry, diag, lo):
        ids = range(len(carry))
        start = pl.multiple_of(j * tk, tk)
        s = [_dot_nt(q[lo + n], _head_rows(k_ref, start, tk, lo + n, heads))
             + (c_q0[lo + n] - c_ref[lo + n, :, pl.ds(start, tk)]) * LOG2E for n in ids]
        if diag:
            s = [jnp.where(causal, x, NEG_INF) for x in s]
        m_new = [jnp.maximum(carry[n][0], jnp.max(s[n], axis=-1, keepdims=True)) for n in ids]
        p = [jnp.exp2(s[n] - m_new[n]) for n in ids]
        alpha = [jnp.exp2(carry[n][0] - m_new[n]) for n in ids]
        pv = [jnp.dot(p[n].astype(BF16), _head_rows(v_ref, start, tk, lo + n, heads),
                      preferred_element_type=F32) for n in ids]
        return tuple((m_new[n], alpha[n] * carry[n][1] + jnp.sum(p[n], axis=-1, keepdims=True),
                      alpha[n] * carry[n][2] + pv[n]) for n in ids)

    init = tuple((jnp.full((tq, 1), NEG_INF, F32), jnp.zeros((tq, 1), F32), jnp.zeros((tq, HEAD_DIM), F32))
                 for _ in hs)
    carry = block(i, init, True)
    carry = lax.fori_loop(0, i, lambda t, cy: block(i - 1 - t, cy, False), carry)
    for g in range(heads):
        o_ref[:, cols[g]] = carry[g][2] / carry[g][1]


def _fox_attention(z, c_rows, batch, seq, tq, tk):
    nq = seq // tq
    kv_spec = lambda which: pl.BlockSpec((None, seq, QKV_BLOCK), lambda b, i: (which, b, 0))
    return pl.pallas_call(
        functools.partial(_fox_kernel, tq=tq, tk=tk, heads=N_FOX),
        grid=(batch, nq),
        in_specs=[pl.BlockSpec((None, tq, QKV_BLOCK), lambda b, i: (3, b * nq + i, 0)), kv_spec(4), kv_spec(5),
                  pl.BlockSpec((N_FOX, 1, seq), lambda b, i: (b, 0, 0))],
        out_specs=pl.BlockSpec((tq, QKV_BLOCK), lambda b, i: (b * nq + i, 0)),
        out_shape=jax.ShapeDtypeStruct((batch * seq, QKV_BLOCK), F32),
        compiler_params=_cparams(("arbitrary", "arbitrary")),
        name="fox_attention",
    )(z, z, z, c_rows)


def _reverse_cumsum_lanes(x, lane):
    n = x.shape[-1]
    d = N_SB
    while d < n:
        shifted = pltpu.roll(x, n - d, 1)
        x = x + jnp.where(lane + d < n, shifted, 0.0)
        d *= 2
    return x


def _decode_kernel(pt_ref, qs_ref, qf_ref, kown_ref, vown_ref, lfown_ref, *refs, pages_per_step):
    del pt_ref
    pp = pages_per_step
    ksb = refs[0:pp]
    vsb = refs[pp:2 * pp]
    kfx = refs[2 * pp:3 * pp]
    vfx = refs[3 * pp:4 * pp]
    lfc = refs[4 * pp:5 * pp]
    osb_ref, ofx_ref = refs[5 * pp:5 * pp + 2]
    acc_sb, acc_fx, run_sb, run_fx, m_ref, l_ref = refs[5 * pp + 2:]
    j = pl.program_id(1)
    page_rows = ksb[0].shape[0] * ksb[0].shape[1]
    rows = pp * page_rows

    qs = qs_ref[...] * SCALE
    qf = qf_ref[...] * SCALE

    @pl.when(j == 0)
    def _():
        acc_sb[...] = jnp.zeros_like(acc_sb)
        run_sb[...] = jnp.zeros_like(run_sb)
        m_ref[...] = jnp.sum(qf * kown_ref[...], axis=-1, keepdims=True)
        l_ref[...] = jnp.ones_like(l_ref)
        acc_fx[...] = vown_ref[...]
        run_fx[...] = lfown_ref[...]

    def rows_of(page_refs):
        return jnp.concatenate([page_refs[p][...].reshape(page_rows, HEAD_DIM).astype(BF16)
                                for p in reversed(range(pp))], axis=0)

    lane = lax.broadcasted_iota(I32, (N_SB, rows), 1)
    head = lax.broadcasted_iota(I32, (N_SB, rows), 0)
    valid = (lane & (N_SB - 1)) == head

    z = _dot_nt(qs.astype(BF16), rows_of(ksb))
    log_stay = jnp.where(valid, _log_sigmoid(-z), 0.0)
    incl = _reverse_cumsum_lanes(log_stay, lane)
    run = run_sb[...]
    a = jnp.where(valid, jnp.exp(z + incl + run), 0.0)
    acc_sb[...] += jnp.dot(a.astype(BF16), rows_of(vsb), preferred_element_type=F32)
    run_sb[...] = run + jnp.sum(log_stay, axis=-1, keepdims=True)

    lf = jnp.where(valid, jnp.concatenate([lfc[p][...] for p in reversed(range(pp))], axis=1), 0.0)
    incl = _reverse_cumsum_lanes(lf, lane)
    run = run_fx[...]
    s = jnp.where(valid, _dot_nt(qf.astype(BF16), rows_of(kfx)) + (incl - lf) + run, NEG_INF)
    m_old = m_ref[...]
    m_new = jnp.maximum(m_old, jnp.max(s, axis=-1, keepdims=True))
    alpha = jnp.exp(m_old - m_new)
    pr = jnp.exp(s - m_new)
    l_ref[...] = alpha * l_ref[...] + jnp.sum(pr, axis=-1, keepdims=True)
    acc_fx[...] = alpha * acc_fx[...] + jnp.dot(pr.astype(BF16), rows_of(vfx), preferred_element_type=F32)
    m_ref[...] = m_new
    run_fx[...] = run + jnp.sum(lf, axis=-1, keepdims=True)

    @pl.when(j == pl.num_programs(1) - 1)
    def _():
        osb_ref[...] = acc_sb[...]
        ofx_ref[...] = acc_fx[...] / l_ref[...]


def _decode_attention(page_table, q_sb, q_fx, k_own, v_own, lf_own, c_sb_k, c_sb_v, c_fx_k, c_fx_v,
                      c_logf_rows, pages_per_step):
    nb, n_pages = page_table.shape
    page, heads, hd = c_sb_k.shape[1:]
    rows = page * heads
    pp = pages_per_step
    steps = n_pages // pp

    def page_of(p):
        return lambda b, j, pt: (pt[b * n_pages + (n_pages - 1 - (j * pp + p))], 0, 0, 0)

    def lf_page_of(p):
        return lambda b, j, pt: (pt[b * n_pages + (n_pages - 1 - (j * pp + p))], 0, 0)

    per_b = pl.BlockSpec((None, heads, hd), lambda b, j, pt: (b, 0, 0))
    cache_specs = [pl.BlockSpec((None, page, heads, hd), page_of(p)) for p in range(pp)]
    lf_specs = [pl.BlockSpec((None, 1, rows), lf_page_of(p)) for p in range(pp)]
    grid_spec = pltpu.PrefetchScalarGridSpec(
        num_scalar_prefetch=1,
        grid=(nb, steps),
        in_specs=[per_b, per_b, per_b, per_b, pl.BlockSpec((None, heads, 1), lambda b, j, pt: (b, 0, 0))]
        + cache_specs * 4 + lf_specs,
        out_specs=[per_b, per_b],
        scratch_shapes=[pltpu.VMEM((heads, hd), F32), pltpu.VMEM((heads, hd), F32),
                        pltpu.VMEM((heads, 1), F32), pltpu.VMEM((heads, 1), F32),
                        pltpu.VMEM((heads, 1), F32), pltpu.VMEM((heads, 1), F32)],
    )
    return pl.pallas_call(
        functools.partial(_decode_kernel, pages_per_step=pp),
        grid_spec=grid_spec,
        out_shape=[jax.ShapeDtypeStruct((nb, heads, hd), F32)] * 2,
        compiler_params=_cparams(("arbitrary", "arbitrary")),
        name="decode_attention",
    )(page_table.reshape(-1), q_sb, q_fx, k_own, v_own, lf_own,
      *([c_sb_k] * pp), *([c_sb_v] * pp), *([c_fx_k] * pp), *([c_fx_v] * pp), *([c_logf_rows] * pp))


ROUTE_EXPERT, ROUTE_GATE, ROUTE_RANK = 0, 2, 4


def _merge_kernel(osb_ref, ofx_ref, x_ref, gsb_ref, gfx_ref, wout_ref, gffn_ref, wr_ref, br_ref,
                  hp_ref, u_ref, route_ref, count_ref, base_ref):
    @pl.when(pl.program_id(0) == 0)
    def _():
        base_ref[...] = jnp.zeros_like(base_ref)

    o = jnp.concatenate([_rms(osb_ref[...], gsb_ref[...]), _rms(ofx_ref[...], gfx_ref[...])], axis=-1)
    y = jnp.dot(o.astype(BF16), wout_ref[...], preferred_element_type=F32)
    hp = x_ref[...] + y
    hp_ref[...] = hp
    u = _rms(hp, gffn_ref[...])
    tm = u.shape[0]
    for s in range(SLAB_ROWS):
        u_ref[s // HALF_ROWS, pl.ds(s % HALF_ROWS, tm, stride=HALF_ROWS), :] = u[:, s * LANES:(s + 1) * LANES]

    stacked = jnp.concatenate(_split_terms(u), axis=0).astype(BF16)
    parts = jnp.dot(stacked, wr_ref[...], preferred_element_type=F32)
    both = parts[:tm] + parts[tm:]
    logits = both[:, :LANES] + both[:, LANES:] + br_ref[...]

    lane = lax.broadcasted_iota(I32, (tm, LANES), 1).astype(F32)
    is_group = (lane >= N_EXPERTS) & (lane < N_EXPERTS + N_GROUPS)
    big = float(1 << 20)
    lg = jnp.where(is_group, logits, NEG_INF)
    lg_max = jnp.max(lg, axis=-1, keepdims=True)
    gsel = jnp.min(jnp.where(lg == lg_max, lane, big), axis=-1, keepdims=True) - N_EXPERTS
    g1 = 1.0 / jnp.sum(jnp.exp(lg - lg_max), axis=-1, keepdims=True)

    in_group = (lane >= gsel * EXPERTS_PER_GROUP) & (lane < (gsel + 1) * EXPERTS_PER_GROUP)
    le = jnp.where(in_group, logits, NEG_INF)
    v_a = jnp.max(le, axis=-1, keepdims=True)
    i_a = jnp.min(jnp.where(le == v_a, lane, big), axis=-1, keepdims=True)
    le_b = jnp.where(lane == i_a, NEG_INF, le)
    v_b = jnp.max(le_b, axis=-1, keepdims=True)
    i_b = jnp.min(jnp.where(le_b == v_b, lane, big), axis=-1, keepdims=True)
    e_b = jnp.exp(v_b - v_a)
    gate_a = g1 / (1.0 + e_b)
    gate_b = g1 * e_b / (1.0 + e_b)

    onehot = jnp.where((lane == i_a) | (lane == i_b), 1.0, 0.0)
    rr = lax.broadcasted_iota(I32, (tm, tm), 0)
    cc = lax.broadcasted_iota(I32, (tm, tm), 1)
    count_dtype = BF16 if tm % 16 == 0 else F32
    earlier = (cc < rr).astype(count_dtype)
    rank_all = jnp.dot(earlier, onehot.astype(count_dtype), preferred_element_type=F32) + base_ref[...]
    rank_a = jnp.sum(jnp.where(lane == i_a, rank_all, 0.0), axis=-1, keepdims=True)
    rank_b = jnp.sum(jnp.where(lane == i_b, rank_all, 0.0), axis=-1, keepdims=True)
    total = base_ref[...] + jnp.sum(onehot, axis=0, keepdims=True)
    base_ref[...] = total
    count_ref[...] = total

    route = jnp.zeros((tm, LANES), F32)
    for k, val in ((ROUTE_EXPERT, i_a), (ROUTE_EXPERT + 1, i_b), (ROUTE_GATE, gate_a), (ROUTE_GATE + 1, gate_b),
                   (ROUTE_RANK, rank_a), (ROUTE_RANK + 1, rank_b)):
        route = jnp.where(lane == k, val, route)
    route_ref[...] = route


def _merge(o_sb, o_fx, x, g_sb, g_fx, w_out, g_ffn, w_r, b_r, tm):
    t, d = x.shape
    row = lambda n: pl.BlockSpec((tm, n), lambda i: (i, 0))
    const = lambda a: pl.BlockSpec(a.shape, lambda i: (0,) * a.ndim)
    return pl.pallas_call(
        _merge_kernel,
        grid=(t // tm,),
        in_specs=[row(o_sb.shape[1]), row(o_fx.shape[1]), row(d), const(g_sb), const(g_fx), const(w_out),
                  const(g_ffn), const(w_r), const(b_r)],
        out_specs=[row(d), pl.BlockSpec((2, tm * HALF_ROWS, LANES), lambda i: (0, i, 0)), row(LANES),
                   pl.BlockSpec((1, LANES), lambda i: (0, 0))],
        out_shape=[jax.ShapeDtypeStruct((t, d), F32), jax.ShapeDtypeStruct((2, t * HALF_ROWS, LANES), F32),
                   jax.ShapeDtypeStruct((t, LANES), F32), jax.ShapeDtypeStruct((1, LANES), F32)],
        scratch_shapes=[pltpu.VMEM((1, LANES), F32)],
        compiler_params=_cparams(("arbitrary",)),
        name="merge_router",
    )(o_sb, o_fx, x, g_sb, g_fx, w_out, g_ffn, w_r, b_r)


def _dispatch_plan(route, counts, tile, n_tiles_max):
    t = route.shape[0]
    expert = route[:, ROUTE_EXPERT:ROUTE_EXPERT + TOP_K].T.astype(I32)
    rank = route[:, ROUTE_RANK:ROUTE_RANK + TOP_K].T.astype(I32)
    count = counts[0, :N_EXPERTS].astype(I32)
    tiles = (count + tile - 1) // tile
    tile_end = jnp.cumsum(tiles)
    first_row = ((tile_end - tiles) * tile).astype(F32)
    row0 = jnp.dot(jax.nn.one_hot(expert, N_EXPERTS, dtype=F32), first_row, precision=lax.Precision.HIGHEST)
    pos = (row0.astype(I32) + rank).reshape(-1)
    tile_ids = jnp.arange(n_tiles_max, dtype=I32)
    tile_expert = jnp.minimum(jnp.sum((tile_end[None, :] <= tile_ids[:, None]).astype(I32), axis=1), N_EXPERTS - 1)
    token = jnp.tile(jnp.arange(t, dtype=I32), TOP_K)
    row_token = jnp.zeros((n_tiles_max * tile,), I32).at[pos].set(token, unique_indices=True)
    return pos, tile_expert, row_token, tile_end[-1:].astype(I32)


def _slab_copy(src_hbm, src_row, dst, dst_row, sem):
    return pltpu.make_async_copy(src_hbm.at[pl.ds(pl.multiple_of(src_row * SLAB_ROWS, SLAB_ROWS), SLAB_ROWS), :],
                                 dst.at[pl.ds(pl.multiple_of(dst_row * SLAB_ROWS, SLAB_ROWS), SLAB_ROWS), :], sem)


def _rows_from_slabs(slabs, n):
    return jnp.concatenate([slabs[pl.ds(s, n, stride=SLAB_ROWS), :] for s in range(SLAB_ROWS)], axis=1)


def _expert_kernel(te_ref, rt_ref, nt_ref, u_hbm, wg_ref, wu_ref, *refs, tile, half):
    final = half == 1
    if final:
        wd_ref, part_ref, out_ref, u_vmem, xs, sem, wg_b, wu_b, wd_b = refs
    else:
        out_ref, u_vmem, xs, sem, wg_b, wu_b = refs
    i = pl.program_id(0)
    nt = nt_ref[0]
    d_e = wg_b.shape[1]

    @pl.when(i == 0)
    def _():
        load = pltpu.make_async_copy(u_hbm.at[half], u_vmem, sem.at[0])
        load.start()
        load.wait()

    @pl.when(i < nt)
    def _():
        @pl.when((i == 0) | (te_ref[i] != te_ref[jnp.maximum(i - 1, 0)]))
        def _():
            wg_b[...] = wg_ref[...].astype(BF16)
            wu_b[...] = wu_ref[...].astype(BF16)
            if final:
                wd_b[...] = wd_ref[...].astype(BF16)

        base = i * tile

        def body(r, carry):
            src = pl.multiple_of(rt_ref[base + r] * HALF_ROWS, HALF_ROWS)
            xs[pl.ds(pl.multiple_of(r * HALF_ROWS, HALF_ROWS), HALF_ROWS), :] = u_vmem[pl.ds(src, HALF_ROWS), :]
            return carry

        lax.fori_loop(0, tile, body, 0, unroll=8)
        x = jnp.concatenate([xs[pl.ds(s, tile, stride=HALF_ROWS), :] for s in range(HALF_ROWS)],
                            axis=1).astype(BF16)
        gate = jnp.dot(x, wg_b[...], preferred_element_type=F32)
        up = jnp.dot(x, wu_b[...], preferred_element_type=F32)
        if final:
            gate = gate + part_ref[:, :d_e]
            up = up + part_ref[:, d_e:]
            h = gate * (1.0 / (1.0 + jnp.exp(-gate))) * up
            y = jnp.dot(h.astype(BF16), wd_b[...], preferred_element_type=F32)
            for s in range(SLAB_ROWS):
                out_ref[pl.ds(s, tile, stride=SLAB_ROWS), :] = y[:, s * LANES:(s + 1) * LANES]
        else:
            out_ref[:, :d_e] = gate
            out_ref[:, d_e:] = up

    @pl.when(i >= nt)
    def _():
        out_ref[...] = jnp.zeros_like(out_ref)


def _experts(u_halves, tile_expert, row_token, n_tiles, w_gate, w_up, w_down, tile):
    n_e, d, d_e = w_gate.shape
    half_d = HALF_ROWS * LANES
    n_tiles_max = tile_expert.shape[0]
    last = lambda i, nt: jnp.minimum(i, nt[0] - 1)

    def call(half, extra_in, extra_specs, out_spec, out_shape, extra_scratch):
        grid_spec = pltpu.PrefetchScalarGridSpec(
            num_scalar_prefetch=3,
            grid=(n_tiles_max,),
            in_specs=[
                pl.BlockSpec(memory_space=pl.ANY),
                pl.BlockSpec((None, half_d, d_e), lambda i, te, rt, nt: (te[last(i, nt)], half, 0)),
                pl.BlockSpec((None, half_d, d_e), lambda i, te, rt, nt: (te[last(i, nt)], half, 0)),
            ] + extra_specs,
            out_specs=out_spec,
            scratch_shapes=[pltpu.VMEM(u_halves.shape[1:], F32), pltpu.VMEM((tile * HALF_ROWS, LANES), F32),
                            pltpu.SemaphoreType.DMA((1,)),
                            pltpu.VMEM((half_d, d_e), BF16), pltpu.VMEM((half_d, d_e), BF16)] + extra_scratch,
        )
        return pl.pallas_call(
            functools.partial(_expert_kernel, tile=tile, half=half),
            grid_spec=grid_spec,
            out_shape=out_shape,
            compiler_params=_cparams(("arbitrary",)),
            name=f"moe_experts_half{half}",
        )(tile_expert, row_token, n_tiles, u_halves, w_gate, w_up, *extra_in)

    part_spec = pl.BlockSpec((tile, 2 * d_e), lambda i, te, rt, nt: (i, 0))
    part = call(0, [], [], part_spec, jax.ShapeDtypeStruct((n_tiles_max * tile, 2 * d_e), F32), [])
    return call(1, [w_down, part],
                [pl.BlockSpec((None, d_e, d), lambda i, te, rt, nt: (te[last(i, nt)], 0, 0)), part_spec],
                pl.BlockSpec((tile * SLAB_ROWS, LANES), lambda i, te, rt, nt: (i, 0)),
                jax.ShapeDtypeStruct((n_tiles_max * tile * SLAB_ROWS, LANES), F32),
                [pltpu.VMEM((d_e, d), BF16)])


def _combine_kernel(pos_ref, rows_hbm, hp_ref, route_ref, gfin_ref, y_ref, buf, sem, *, tm):
    i = pl.program_id(0)
    n = pl.num_programs(0)
    slot = lax.rem(i, 2)

    def gather(tile_idx, dst_slot):
        base = tile_idx * tm

        def body(r, carry):
            for k in range(TOP_K):
                _slab_copy(rows_hbm, pos_ref[k * (n * tm) + base + r], buf.at[dst_slot, k], r,
                           sem.at[dst_slot]).start()
            return carry

        lax.fori_loop(0, tm, body, 0, unroll=4)

    @pl.when(i == 0)
    def _():
        gather(0, 0)

    @pl.when(i + 1 < n)
    def _():
        gather(i + 1, 1 - slot)

    pltpu.make_async_copy(buf.at[slot], buf.at[slot], sem.at[slot]).wait()
    route = route_ref[...]
    g_a = route[:, ROUTE_GATE:ROUTE_GATE + 1]
    g_b = route[:, ROUTE_GATE + 1:ROUTE_GATE + 2]
    f = g_a * _rows_from_slabs(buf.at[slot, 0], tm) + g_b * _rows_from_slabs(buf.at[slot, 1], tm)
    y_ref[...] = _rms(hp_ref[...] + f, gfin_ref[...])


def _combine(pos, rows, hp, route, g_final, tm):
    t, d = hp.shape
    grid_spec = pltpu.PrefetchScalarGridSpec(
        num_scalar_prefetch=1,
        grid=(t // tm,),
        in_specs=[
            pl.BlockSpec(memory_space=pl.ANY),
            pl.BlockSpec((tm, d), lambda i, pos: (i, 0)),
            pl.BlockSpec((tm, LANES), lambda i, pos: (i, 0)),
            pl.BlockSpec((1, d), lambda i, pos: (0, 0)),
        ],
        out_specs=pl.BlockSpec((tm, d), lambda i, pos: (i, 0)),
        scratch_shapes=[pltpu.VMEM((2, TOP_K, tm * SLAB_ROWS, LANES), F32), pltpu.SemaphoreType.DMA((2,))],
    )
    return pl.pallas_call(
        functools.partial(_combine_kernel, tm=tm),
        grid_spec=grid_spec,
        out_shape=jax.ShapeDtypeStruct((t, d), F32),
        compiler_params=_cparams(("arbitrary",)),
        name="moe_combine",
    )(pos, rows, hp, route, g_final)


def _routed_moe(u, hp, route, counts, w_gate, w_up, w_down, g_final, tile, tm_combine):
    t = hp.shape[0]
    n_tiles_max = (TOP_K * t + N_EXPERTS * (tile - 1)) // tile
    pos, tile_expert, row_token, n_tiles = _dispatch_plan(route, counts, tile, n_tiles_max)
    rows = _experts(u, tile_expert, row_token, n_tiles, w_gate, w_up, w_down, tile)
    return _combine(pos, rows, hp, route, g_final, tm_combine)


def _pad_lanes(a, n=LANES):
    return jnp.pad(a, ((0, 0), (0, n - a.shape[1])))


def kernel(x_prompt, x_sample, cache_sb_k, cache_sb_v, cache_fox_k, cache_fox_v, cache_fox_logf, page_table, norm_attn_g, w_in, b_forget, g_sb_out, g_fox_out, w_out, norm_ffn_g, w_router_group, b_router_group, w_router_expert, b_router_expert, w_expert_gate, w_expert_up, w_expert_down, norm_final_g):
    batch, seq, d = x_prompt.shape
    nb = x_sample.shape[0]
    assert w_in.shape[0] == 1, "one trunk layer"
    d_qkv = 3 * (N_SB + N_FOX) * HEAD_DIM

    w_main = w_in[0, :, :d_qkv].astype(BF16)
    w_f = _pad_lanes(w_in[0, :, d_qkv:]).astype(BF16)
    b_f = _pad_lanes(b_forget)
    g_attn = norm_attn_g
    w_out_b = w_out[0].astype(BF16)
    w_r = jnp.concatenate([w_router_expert[0].transpose(1, 0, 2).reshape(d, N_EXPERTS), w_router_group[0]], axis=1)
    w_r = _pad_lanes(w_r)
    w_r_hi = w_r.astype(BF16)
    w_r_lo = (w_r - w_r_hi.astype(F32)).astype(BF16)
    w_r2 = jnp.concatenate([w_r_hi, w_r_lo], axis=1)
    b_r = _pad_lanes(jnp.concatenate([b_router_expert[0].reshape(1, N_EXPERTS), b_router_group], axis=1))
    g_fin = norm_final_g.reshape(1, d)
    experts = (w_expert_gate[0], w_expert_up[0], w_expert_down[0])

    xp = x_prompt.reshape(batch * seq, d)
    z_p, logf_p, *kv_p = _inproj(xp, g_attn, w_main, w_f, b_f, tm=TM_INPROJ)
    logf_p = logf_p[:, :N_FOX].reshape(batch, seq, N_FOX)
    c_rows = _cumsum_rows(logf_p.transpose(0, 2, 1)).reshape(batch * N_FOX, 1, seq)
    o_sb = _sb_attention(z_p, batch, seq, TQ, TK)
    o_fx = _fox_attention(z_p, c_rows, batch, seq, TQ, TK)
    hp_p, u_p, route_p, count_p = _merge(o_sb, o_fx, xp, g_sb_out, g_fox_out, w_out_b, norm_ffn_g, w_r2, b_r,
                                         tm=TM_MERGE)
    y_prompt = _routed_moe(u_p, hp_p, route_p, count_p, *experts, g_fin, TILE_EXPERT, TM_COMBINE)
    y_prompt = y_prompt.reshape(batch, seq, d)

    xs = x_sample.reshape(nb, d)
    z_s, logf_s, *kv_s = _inproj(xs, g_attn, w_main, w_f, b_f, tm=nb)
    logf_s = logf_s[:, :N_FOX]
    per_head = lambda a: a.reshape(nb, N_SB, HEAD_DIM)
    page = cache_fox_logf.shape[2]
    logf_rows = cache_fox_logf[0].reshape(-1, 1, page * N_FOX)
    o_sb_s, o_fx_s = _decode_attention(
        page_table, per_head(z_s[0]), per_head(z_s[3]), per_head(kv_s[2]), per_head(kv_s[3]),
        logf_s.reshape(nb, N_FOX, 1), cache_sb_k[0], cache_sb_v[0], cache_fox_k[0], cache_fox_v[0],
        logf_rows, PAGES_PER_STEP)
    hp_s, u_s, route_s, count_s = _merge(o_sb_s.reshape(nb, -1), o_fx_s.reshape(nb, -1), xs, g_sb_out, g_fox_out,
                                         w_out_b, norm_ffn_g, w_r2, b_r, tm=nb)
    y_sample = _routed_moe(u_s, hp_s, route_s, count_s, *experts, g_fin, nb, nb).reshape(nb, 1, d)

    heads_p = lambda a: a.reshape(1, batch, seq, N_SB, HEAD_DIM)
    heads_s = lambda a: a.reshape(1, nb, 1, N_SB, HEAD_DIM)
    return (y_prompt, y_sample,
            heads_p(kv_p[0]), heads_p(kv_p[1]), heads_p(kv_p[2]), heads_p(kv_p[3]), logf_p[None],
            heads_s(kv_s[0]), heads_s(kv_s[1]), heads_s(kv_s[2]), heads_s(kv_s[3]),
            logf_s.reshape(1, nb, 1, N_FOX))
```
